```python
import math
import jax
import jax.numpy as jnp
from jax import lax
import numpy as np

D_MODEL = 2048
BATCH = 8
SEQ = 2048
DEPTH = 2
DEC_BATCH = 32
DEC_SEQ = 4
PAST_LEN = 8192
PAGE_SIZE = 128

N_A = DEPTH // 2
N_B = DEPTH - N_A
HEAD_DIM = 128
MIX_WIDTH = D_MODEL
MEM_HEADS = 4
MEM_LEN = 256
MEM_WIDTH = MEM_HEADS * HEAD_DIM
TOK_WIDTH = MIX_WIDTH - MEM_WIDTH
H_A = TOK_WIDTH // HEAD_DIM
DK_A = HEAD_DIM
DV_A = HEAD_DIM
HGRN_CHUNK = 32
H_B = TOK_WIDTH // HEAD_DIM
G_B = 2
HPG = H_B // G_B
CMP_BLOCK = 32
CMP_STRIDE = 16
SEL_BLOCK = 64
N_SEL = 16
WINDOW = 512
SEL_QBLOCK = 16
WIN_QBLOCK = 128
FORCE_BONUS = 1e4
D_FF = 11 * D_MODEL // 4
CONV_W = 3
N_KV_ROWS = 4
EPS = 1e-6
NEG = -1e30
SCALE = HEAD_DIM ** -0.5

kernel_name = 'yoco_hgrn2_nsa_memory_step'


def rmsnorm(x, g):
    xf = x.astype(jnp.float32)
    y = xf * lax.rsqrt(jnp.mean(xf * xf, axis=-1, keepdims=True) + EPS)
    return (y * g.astype(jnp.float32)).astype(x.dtype)


def masked_softmax(s, mask):
    s = jnp.where(mask, s, NEG)
    m = jnp.max(s, axis=-1, keepdims=True)
    p = jnp.exp(s - m) * mask
    return p / jnp.maximum(jnp.sum(p, axis=-1, keepdims=True), 1e-30)


def alibi_slopes():
    h = np.arange(1, H_B + 1, dtype=np.float32)
    return jnp.asarray(np.exp2(-8.0 * h / H_B).astype(np.float32)).reshape(G_B, HPG)


def hgrn2_chunked(q, k, v, logf, s0):
    B, T, H, _ = q.shape
    C = math.gcd(T, HGRN_CHUNK)
    n = T // C

    def chunks(a):
        return a.reshape(B, n, C, *a.shape[2:]).swapaxes(0, 1)

    causal = jnp.tril(jnp.ones((C, C), dtype=bool))[None, :, :, None, None]

    def step(S, inp):
        qc, kc, vc, lc = inp
        b = jnp.cumsum(lc, axis=1)
        diff = b[:, :, None] - b[:, None, :]
        decay = jnp.where(causal, jnp.exp(jnp.where(causal, diff, 0.0)), 0.0)
        a = jnp.einsum('bthk,bshk,btshk->btsh', qc, kc, decay)
        o = jnp.einsum('btsh,bshv->bthv', a, vc) + jnp.einsum('bthk,bhkv->bthv', qc * jnp.exp(b), S)
        b_last = b[:, -1]
        S = jnp.exp(b_last)[..., None] * S + jnp.einsum(
            'bshk,bshv->bhkv', kc * jnp.exp(b_last[:, None] - b), vc)
        return S, o

    S, o = lax.scan(step, s0, (chunks(q), chunks(k), chunks(v), chunks(logf)))
    return o.swapaxes(0, 1).reshape(B, T, H, -1), S


def hgrn2_mixer(xn, w_in, lb, gnorm, s0):
    B, T, _ = xn.shape
    W = TOK_WIDTH
    u = xn @ w_in

    def hs(a):
        return a.reshape(B, T, H_A, -1)

    q = hs(jax.nn.silu(u[..., :W].astype(jnp.float32)))
    fg = lb + (1.0 - lb) * jax.nn.sigmoid(u[..., W:2 * W].astype(jnp.float32))
    k = hs(1.0 - fg)
    logf = hs(jnp.log(fg))
    v = hs(u[..., 2 * W:3 * W].astype(jnp.float32))
    og = hs(jax.nn.sigmoid(u[..., 3 * W:4 * W].astype(jnp.float32)))
    mem_q = u[..., 4 * W:].reshape(B, T, MEM_HEADS, HEAD_DIM)
    o, s_fin = hgrn2_chunked(q, k, v, logf, s0.astype(jnp.float32))
    o = rmsnorm(o, gnorm) * og
    return o.reshape(B, T, W).astype(xn.dtype), mem_q, s_fin


def compress(seq, pe, w1, w2):
    B, L = seq.shape[:2]
    r = CMP_BLOCK // CMP_STRIDE
    nch = L // CMP_STRIDE
    nc = nch - r + 1
    ch = seq[:, :nch * CMP_STRIDE].reshape(B, nch, CMP_STRIDE, G_B, HEAD_DIM)
    w1r = w1.reshape(r, CMP_STRIDE, HEAD_DIM, HEAD_DIM)
    part = jnp.einsum('bcsgd,jsdh->jbcgh', ch, w1r)
    pre = jnp.einsum('jsd,jsdh->h', pe.reshape(r, CMP_STRIDE, HEAD_DIM), w1r)
    for j in range(r):
        pre = pre + part[j][:, j:j + nc]
    return jax.nn.gelu(pre) @ w2


def shared_kv(h, kv_past, win_past, prm):
    B, T, _ = h.shape
    kv = (rmsnorm(h, prm['kv_norm']) @ prm['w_kv_b']).reshape(B, T, N_KV_ROWS + 2, G_B, HEAD_DIM)
    rows, win_rows = kv[:, :, :N_KV_ROWS], kv[:, :, N_KV_ROWS:]
    if kv_past is None:
        full = rows
        nb = T // WIN_QBLOCK
        kidx = jnp.arange(nb)[:, None] * WIN_QBLOCK + jnp.arange(WIN_QBLOCK + WINDOW)[None, :]
        wpad = jnp.pad(win_rows, ((0, 0), (WINDOW, 0), (0, 0), (0, 0), (0, 0)))
        wblk = wpad[:, kidx]
        wkpos = kidx - WINDOW
    else:
        past = kv_past.shape[1]
        full = jnp.concatenate([kv_past.astype(rows.dtype), rows], axis=1)
        wb = win_past.shape[1]
        wblk = jnp.concatenate([win_past.astype(win_rows.dtype), win_rows], axis=1)[:, None]
        wkpos = (past - wb + jnp.arange(wb + T))[None, :]
    L = full.shape[1]
    ns = -(-L // SEL_BLOCK)
    kc = compress(full[:, :, 0], prm['cmp_pos'][0], prm['w_cmp1'][0], prm['w_cmp2'][0])
    vc = compress(full[:, :, 1], prm['cmp_pos'][1], prm['w_cmp1'][1], prm['w_cmp2'][1])
    sel = jnp.pad(full[:, :, 2:4], ((0, 0), (0, ns * SEL_BLOCK - L), (0, 0), (0, 0), (0, 0)))
    sel = sel.reshape(B, ns, SEL_BLOCK, 2, G_B, HEAD_DIM)
    shared = dict(kc=kc, vc=vc, kb=sel[:, :, :, 0], vb=sel[:, :, :, 1],
                  wk=wblk[:, :, :, 0], wv=wblk[:, :, :, 1], wkpos=wkpos)
    return shared, rows, win_rows


def cmp_branch(q, qpos, kc, vc, slopes):
    nc = kc.shape[1]
    s = jnp.einsum('btgrd,bigd->btgri', q, kc).astype(jnp.float32) * SCALE
    end = jnp.arange(nc) * CMP_STRIDE + CMP_BLOCK - 1
    dist = qpos[:, None] - end[None, :]
    mask = (dist >= 0)[None, :, None, None, :]
    s = s - slopes[None, None, :, :, None] * dist.astype(jnp.float32)[None, :, None, None, :]
    p = masked_softmax(s, mask)
    o = jnp.einsum('btgri,bigd->btgrd', p.astype(vc.dtype), vc)
    return o, p.sum(axis=3)


def cmp_to_sel(p, ns):
    r = SEL_BLOCK // CMP_STRIDE
    back = CMP_BLOCK // CMP_STRIDE - 1
    nc = p.shape[-1]
    pp = jnp.pad(p, [(0, 0)] * (p.ndim - 1) + [(back, r * ns - nc)])
    out = pp[..., 0:r * (ns - 1) + 1:r]
    for o in range(1, back + r):
        out = out + pp[..., o:o + r * (ns - 1) + 1:r]
    return out


def select_blocks(imp, qpos, ns):
    p = cmp_to_sel(imp, ns)
    j = jnp.arange(ns)[None, :]
    cur = (qpos // SEL_BLOCK)[:, None]
    valid = (j <= cur)[None, :, None, :]
    forced = ((j == 0) | (j == cur) | (j == cur - 1))[None, :, None, :]
    score = jnp.where(valid, p + jnp.where(forced, FORCE_BONUS, 0.0), NEG)
    _, idx = lax.top_k(score, min(N_SEL, ns))
    ok = idx <= cur[None, :, :, None]
    return idx, ok


def sel_branch(q, qpos, idx, ok, kb, vb, slopes):
    B, T = q.shape[:2]
    qb = math.gcd(T, SEL_QBLOCK)
    nq = T // qb

    def split(a):
        return a.reshape(B, nq, qb, *a.shape[2:]).swapaxes(0, 1)

    bi = jnp.arange(B)[:, None, None, None]
    gi = jnp.arange(G_B)[None, None, :, None]

    def blk(args):
        qq, pp, ii, oo = args
        kk = kb[bi, ii, :, gi]
        vv = vb[bi, ii, :, gi]
        n = ii.shape[-1]
        s = jnp.einsum('bqgrd,bqgnkd->bqgrnk', qq, kk).astype(jnp.float32) * SCALE
        kpos = ii[..., None] * SEL_BLOCK + jnp.arange(SEL_BLOCK)
        dist = pp[None, :, None, None, None] - kpos
        mask = oo[..., None] & (dist >= 0)
        s = s - slopes[None, None, :, :, None, None] * dist.astype(jnp.float32)[:, :, :, None]
        s = s.reshape(B, qb, G_B, HPG, n * SEL_BLOCK)
        p = masked_softmax(s, mask.reshape(B, qb, G_B, 1, n * SEL_BLOCK))
        p = p.reshape(B, qb, G_B, HPG, n, SEL_BLOCK)
        return jnp.einsum('bqgrnk,bqgnkd->bqgrd', p.astype(vv.dtype), vv)

    o = lax.map(blk, (split(q), qpos.reshape(nq, qb), split(idx), split(ok)))
    return o.swapaxes(0, 1).reshape(B, T, G_B, HPG, HEAD_DIM)


def win_branch(q, qpos, wk, wv, wkpos, slopes):
    B, T = q.shape[:2]
    nb = wkpos.shape[0]
    qb = T // nb
    qs = q.reshape(B, nb, qb, G_B, HPG, HEAD_DIM).swapaxes(0, 1)

    def blk(args):
        qq, pp, kk, vv, kp = args
        s = jnp.einsum('bqgrd,bkgd->bqgrk', qq, kk).astype(jnp.float32) * SCALE
        dist = pp[:, None] - kp[None, :]
        mask = (dist >= 0) & (dist < WINDOW) & (kp[None, :] >= 0)
        s = s - slopes[None, None, :, :, None] * dist.astype(jnp.float32)[None, :, None, None, :]
        p = masked_softmax(s, mask[None, :, None, None, :])
        return jnp.einsum('bqgrk,bkgd->bqgrd', p.astype(vv.dtype), vv)

    o = lax.map(blk, (qs, qpos.reshape(nb, qb), wk.swapaxes(0, 1), wv.swapaxes(0, 1), wkpos))
    return o.swapaxes(0, 1).reshape(B, T, G_B, HPG, HEAD_DIM)


def nsa_mixer(xn, w_in, sh, qpos, slopes):
    B, T, _ = xn.shape
    u = xn @ w_in
    q = u[..., :TOK_WIDTH].reshape(B, T, G_B, HPG, HEAD_DIM)
    gates = jax.nn.sigmoid(u[..., TOK_WIDTH:TOK_WIDTH + 3 * H_B].astype(jnp.float32))
    gates = gates.reshape(B, T, G_B, HPG, 3)
    mem_q = u[..., TOK_WIDTH + 3 * H_B:].reshape(B, T, MEM_HEADS, HEAD_DIM)
    o_cmp, imp = cmp_branch(q, qpos, sh['kc'], sh['vc'], slopes)
    idx, ok = select_blocks(imp, qpos, sh['kb'].shape[1])
    o_sel = sel_branch(q, qpos, idx, ok, sh['kb'], sh['vb'], slopes)
    o_win = win_branch(q, qpos, sh['wk'], sh['wv'], sh['wkpos'], slopes)
    o = (gates[..., 0:1] * o_cmp.astype(jnp.float32) + gates[..., 1:2] * o_sel.astype(jnp.float32)
         + gates[..., 2:3] * o_win.astype(jnp.float32))
    return o.reshape(B, T, TOK_WIDTH).astype(xn.dtype), mem_q


def mem_attend(q, mkv):
    B, T = q.shape[:2]
    k, v = mkv[:, :, 0], mkv[:, :, 1]
    s = jnp.einsum('bthd,bmhd->bthm', q, k.astype(q.dtype)).astype(jnp.float32) * SCALE
    p = jax.nn.softmax(s, axis=-1)
    return jnp.einsum('bthm,bmhd->bthd', p.astype(q.dtype), v.astype(q.dtype)).reshape(B, T, MEM_WIDTH)


def conv_ffn(x, buf, w_in, w_conv, b_conv, w_out):
    T = x.shape[1]
    u = x @ w_in
    a, b = u[..., :D_FF], u[..., D_FF:]
    ext = jnp.concatenate([buf.astype(a.dtype), a], axis=1)
    c = b_conv
    for j in range(CONV_W):
        c = c + ext[:, j:j + T] * w_conv[j]
    y = (jax.nn.gelu(c) * b) @ w_out
    return y, ext[:, ext.shape[1] - (CONV_W - 1):]


def trunk(x, qpos, mem_kv, hgrn_s0, conv_buf0, kv_past, win_past, prm):
    B = x.shape[0]
    slopes = alibi_slopes()
    lbs = jnp.cumsum(jax.nn.softmax(prm['lb_logits'].astype(jnp.float32), axis=0), axis=0)
    h = x
    shared, kv_rows, win_rows = None, None, None
    new_s, new_buf = [], []
    for l in range(DEPTH):
        g = prm['norm_gains'][l]
        xn = rmsnorm(h, g[0])
        if l < N_A:
            s0 = (jnp.zeros((B, H_A, DK_A, DV_A), jnp.float32) if hgrn_s0 is None else hgrn_s0[l])
            mix, mem_q, s_fin = hgrn2_mixer(xn, prm['w_in_a'][l], lbs[l], prm['hgrn_norm'][l], s0)
            new_s.append(s_fin)
        else:
            mix, mem_q = nsa_mixer(xn, prm['w_in_b'][l - N_A], shared, qpos, slopes)
        mo = mem_attend(mem_q, mem_kv[l])
        o = jnp.concatenate([mix, mo.astype(mix.dtype)], axis=-1) @ prm['w_o'][l]
        h = h + rmsnorm(o, g[1])
        buf0 = (jnp.zeros((B, CONV_W - 1, D_FF), h.dtype) if conv_buf0 is None else conv_buf0[l])
        f, nb = conv_ffn(rmsnorm(h, g[2]), buf0, prm['w_ffn_in'][l], prm['w_ffn_conv'][l],
                         prm['b_ffn_conv'][l], prm['w_ffn_out'][l])
        new_buf.append(nb)
        h = h + rmsnorm(f, g[3])
        if l == N_A - 1:
            shared, kv_rows, win_rows = shared_kv(h, kv_past, win_past, prm)
    return h, jnp.stack(new_s), jnp.stack(new_buf), kv_rows, win_rows


def setup_inputs(seed: int = 0) -> dict:
    key = jax.random.key(seed)
    ks = jax.random.split(key, 26)
    n_pages = PAST_LEN // PAGE_SIZE
    n_phys = (DEC_BATCH * n_pages * 5) // 4
    wb = min(WINDOW, PAST_LEN)

    def nrm(k, shape, scale=1.0):
        return jax.random.normal(k, shape, jnp.float32) * scale

    page_table = jax.random.permutation(ks[8], n_phys)[:DEC_BATCH * n_pages]
    page_table = page_table.reshape(DEC_BATCH, n_pages).astype(jnp.int32)
    return {
        'x_prompt': nrm(ks[0], (BATCH, SEQ, D_MODEL)),
        'x_sample': nrm(ks[1], (DEC_BATCH, DEC_SEQ, D_MODEL)),
        'mem_prompt': nrm(ks[2], (BATCH, MEM_LEN, D_MODEL)),
        'state_hgrn': nrm(ks[3], (N_A, DEC_BATCH, H_A, DK_A, DV_A), 0.5),
        'cache_conv': nrm(ks[4], (DEPTH, DEC_BATCH, CONV_W - 1, D_FF)),
        'cache_mem': nrm(ks[5], (DEPTH, DEC_BATCH, MEM_LEN, 2, MEM_HEADS, HEAD_DIM)),
        'cache_kv': nrm(ks[6], (n_phys, PAGE_SIZE, N_KV_ROWS, G_B, HEAD_DIM)),
        'cache_win': nrm(ks[7], (DEC_BATCH, wb, 2, G_B, HEAD_DIM)),
        'page_table': page_table,
        'norm_gains': 1.0 + nrm(ks[9], (DEPTH, 4, D_MODEL), 0.05),
        'w_in_a': nrm(ks[10], (N_A, D_MODEL, 4 * TOK_WIDTH + MEM_WIDTH), D_MODEL ** -0.5),
        'lb_logits': nrm(ks[11], (N_A + 1, TOK_WIDTH), 0.5),
        'hgrn_norm': 1.0 + nrm(ks[12], (N_A, H_A, DV_A), 0.05),
        'w_in_b': nrm(ks[13], (N_B, D_MODEL, TOK_WIDTH + 3 * H_B + MEM_WIDTH), D_MODEL ** -0.5),
        'w_o': nrm(ks[14], (DEPTH, MIX_WIDTH, D_MODEL), MIX_WIDTH ** -0.5),
        'w_mem_kv': nrm(ks[15], (DEPTH, D_MODEL, 2 * MEM_WIDTH), D_MODEL ** -0.5),
        'kv_norm': 1.0 + nrm(ks[16], (D_MODEL,), 0.05),
        'w_kv_b': nrm(ks[17], (D_MODEL, (N_KV_ROWS + 2) * G_B * HEAD_DIM), D_MODEL ** -0.5),
        'cmp_pos': nrm(ks[18], (2, CMP_BLOCK, HEAD_DIM), 0.5),
        'w_cmp1': nrm(ks[19], (2, CMP_BLOCK * HEAD_DIM, HEAD_DIM), (CMP_BLOCK * HEAD_DIM) ** -0.5),
        'w_cmp2': nrm(ks[20], (2, HEAD_DIM, HEAD_DIM), HEAD_DIM ** -0.5),
        'w_ffn_in': nrm(ks[21], (DEPTH, D_MODEL, 2 * D_FF), D_MODEL ** -0.5),
        'w_ffn_conv': nrm(ks[22], (DEPTH, CONV_W, D_FF), CONV_W ** -0.5),
        'b_ffn_conv': nrm(ks[23], (DEPTH, D_FF), 0.01),
        'w_ffn_out': nrm(ks[24], (DEPTH, D_FF, D_MODEL), D_FF ** -0.5),
    }


def reference(x_prompt, x_sample, mem_prompt, state_hgrn, cache_conv, cache_mem, cache_kv, cache_win,
              page_table, norm_gains, w_in_a, lb_logits, hgrn_norm, w_in_b, w_o, w_mem_kv, kv_norm,
              w_kv_b, cmp_pos, w_cmp1, w_cmp2, w_ffn_in, w_ffn_conv, b_ffn_conv, w_ffn_out):
    prm = dict(norm_gains=norm_gains, w_in_a=w_in_a, lb_logits=lb_logits, hgrn_norm=hgrn_norm,
               w_in_b=w_in_b, w_o=w_o, kv_norm=kv_norm, w_kv_b=w_kv_b, cmp_pos=cmp_pos,
               w_cmp1=w_cmp1, w_cmp2=w_cmp2, w_ffn_in=w_ffn_in, w_ffn_conv=w_ffn_conv,
               b_ffn_conv=b_ffn_conv, w_ffn_out=w_ffn_out)
    bp, tp = x_prompt.shape[:2]
    bs, ts = x_sample.shape[:2]
    mem_kv_prompt = jnp.einsum('bmd,ldk->lbmk', mem_prompt, w_mem_kv)
    mem_kv_prompt = mem_kv_prompt.reshape(DEPTH, bp, MEM_LEN, 2, MEM_HEADS, HEAD_DIM)
    y_prompt, hgrn_p, conv_p, kv_p, win_p = trunk(
        x_prompt, jnp.arange(tp, dtype=jnp.int32), mem_kv_prompt, None, None, None, None, prm)
    past_len = page_table.shape[1] * PAGE_SIZE
    kv_past = cache_kv[page_table].reshape(bs, past_len, N_KV_ROWS, G_B, HEAD_DIM)
    y_sample, hgrn_s, conv_s, kv_s, win_s = trunk(
        x_sample, past_len + jnp.arange(ts, dtype=jnp.int32), cache_mem, state_hgrn, cache_conv,
        kv_past, cache_win, prm)
    kv_prompt_pages = kv_p.reshape(bp, tp // PAGE_SIZE, PAGE_SIZE, N_KV_ROWS, G_B, HEAD_DIM)
    win_prompt = win_p[:, tp - min(WINDOW, tp):]
    return (y_prompt, y_sample, hgrn_p, hgrn_s, conv_p, conv_s, mem_kv_prompt,
            kv_prompt_pages, kv_s, win_prompt, win_s)
```

```python
import functools
import math

import numpy as np
import jax
import jax.numpy as jnp
from jax import lax
from jax.experimental import pallas as pl
from jax.experimental.pallas import tpu as pltpu

F32 = jnp.float32
BF16 = jnp.bfloat16

HEAD_DIM = 128
MEM_HEADS = 4
MEM_WIDTH = MEM_HEADS * HEAD_DIM
G_B = 2
HGRN_CHUNK = 32
CMP_BLOCK = 32
CMP_STRIDE = 16
SEL_BLOCK = 64
N_SEL = 16
WINDOW = 512
PAGE_SIZE = 128
FORCE_BONUS = 1e4
CONV_W = 3
EPS = 1e-6
NEG = -1e30
SCALE = HEAD_DIM ** -0.5
SUBLANES = 8
VMEM_LIMIT = 56 * 1024 * 1024


def _cparams(sem):
    return pltpu.CompilerParams(dimension_semantics=sem, vmem_limit_bytes=VMEM_LIMIT)


def _dot(a, b):
    return jnp.dot(a, b, preferred_element_type=F32)


def _dot_nt(a, b):
    return lax.dot_general(a, b, (((1,), (1,)), ((), ())), preferred_element_type=F32)


def _dot_tn(a, b):
    return lax.dot_general(a, b, (((0,), (0,)), ((), ())), preferred_element_type=F32)


BF16_ROWS = 16


def _pad_rows(x):
    n = x.shape[0]
    if n % BF16_ROWS == 0:
        return x
    return jnp.concatenate([x, jnp.zeros((BF16_ROWS - n % BF16_ROWS,) + x.shape[1:], x.dtype)], axis=0)


def _dot_fewrows(a, b):
    return _dot(_pad_rows(a).astype(BF16), b)[:a.shape[0]]


def _rms(x, g):
    return x * lax.rsqrt(jnp.mean(x * x, axis=-1, keepdims=True) + EPS) * g


def _sigmoid(x):
    return 1.0 / (1.0 + jnp.exp(-x))


def _gelu(x):
    return 0.5 * x * (1.0 + jnp.tanh(0.7978845608028654 * (x + 0.044715 * (x * x * x))))


def _pick(n, cands):
    for c in cands:
        if n % c == 0:
            return c
    raise ValueError(f"no tile for {n} in {cands}")


def _mm_kernel(x_ref, g_ref, w_ref, o_ref, xn_ref, *, norm, rows):
    tm = x_ref.shape[0]

    @pl.when(pl.program_id(1) == 0)
    def _():
        def body(c, carry):
            r = pl.multiple_of(c * rows, rows)
            x = x_ref[pl.ds(r, rows), :]
            if norm:
                x = _rms(x, g_ref[...])
            xn_ref[pl.ds(r, rows), :] = x.astype(BF16)
            return carry
        lax.fori_loop(0, tm // rows, body, 0)

    o_ref[...] = _dot(xn_ref[...], w_ref[...])


def _mm(x, g, w, *, norm=True):
    M, K = x.shape
    N = w.shape[1]
    tm = _pick(M, (1024, 512, 256, 128, 64, 32, 16))
    tn = _pick(N, (512, 768, 256, 128))
    rows = min(tm, 128)
    if g is None:
        g = jnp.ones((K,), F32)
    return pl.pallas_call(
        functools.partial(_mm_kernel, norm=norm, rows=rows),
        grid=(M // tm, N // tn),
        in_specs=[pl.BlockSpec((tm, K), lambda i, j: (i, 0)),
                  pl.BlockSpec((1, K), lambda i, j: (0, 0)),
                  pl.BlockSpec((K, tn), lambda i, j: (0, j))],
        out_specs=pl.BlockSpec((tm, tn), lambda i, j: (i, j)),
        out_shape=jax.ShapeDtypeStruct((M, N), F32),
        scratch_shapes=[pltpu.VMEM((tm, K), BF16)],
        compiler_params=_cparams(("parallel", "arbitrary")),
    )(x, g.reshape(1, K), w)


def _mmres_kernel(*refs, nk1, nk, two):
    if two:
        a1_ref, a2_ref, w_ref, h_ref, g_ref, o_ref = refs
    else:
        a1_ref, w_ref, h_ref, g_ref, o_ref = refs
        a2_ref = None
    k = pl.program_id(1)

    def accum(a_ref):
        part = _dot(a_ref[...].astype(BF16), w_ref[...])

        @pl.when(k == 0)
        def _():
            o_ref[...] = part

        @pl.when(k > 0)
        def _():
            o_ref[...] += part

    if two:
        @pl.when(k < nk1)
        def _():
            accum(a1_ref)

        @pl.when(k >= nk1)
        def _():
            accum(a2_ref)
    else:
        accum(a1_ref)

    @pl.when(k == nk - 1)
    def _():
        o_ref[...] = h_ref[...] + _rms(o_ref[...], g_ref[...])


def _mmres(a1, a2, w, h, g):
    M, N = h.shape
    K = w.shape[0]
    tk = 512
    tm = _pick(M, (512, 256, 128, 64, 32, 16))
    nk = K // tk
    two = a2 is not None
    nk1 = a1.shape[1] // tk
    assert a1.shape[1] % tk == 0 and K % tk == 0
    in_specs = [pl.BlockSpec((tm, tk), lambda i, k: (i, jnp.minimum(k, nk1 - 1)))]
    args = [a1]
    if two:
        assert a2.shape[1] % tk == 0 and a1.shape[1] + a2.shape[1] == K
        in_specs.append(pl.BlockSpec((tm, tk), lambda i, k: (i, jnp.maximum(k - nk1, 0))))
        args.append(a2)
    in_specs += [pl.BlockSpec((tk, N), lambda i, k: (k, 0)),
                 pl.BlockSpec((tm, N), lambda i, k: (i, 0)),
                 pl.BlockSpec((1, N), lambda i, k: (0, 0))]
    args += [w, h, g.reshape(1, N)]
    return pl.pallas_call(
        functools.partial(_mmres_kernel, nk1=nk1, nk=nk, two=two),
        grid=(M // tm, nk),
        in_specs=in_specs,
        out_specs=pl.BlockSpec((tm, N), lambda i, k: (i, 0)),
        out_shape=jax.ShapeDtypeStruct((M, N), F32),
        compiler_params=_cparams(("parallel", "arbitrary")),
    )(*args)


def _ffn_in_kernel(x_ref, g_ref, wa_ref, wb_ref, wc_ref, bc_ref, p1_ref, p2_ref,
                   y_ref, a_ref, xn_ref, carry_ref, *, T, rows):
    tm = x_ref.shape[0]
    i = pl.program_id(0)
    j = pl.program_id(1)

    @pl.when(j == 0)
    def _():
        def body(c, carry):
            r = pl.multiple_of(c * rows, rows)
            xn_ref[pl.ds(r, rows), :] = _rms(x_ref[pl.ds(r, rows), :], g_ref[...]).astype(BF16)
            return carry
        lax.fori_loop(0, tm // rows, body, 0)

    xn = xn_ref[...]
    a = _dot(xn, wa_ref[...])
    b = _dot(xn, wb_ref[...])
    row = lax.broadcasted_iota(jnp.int32, (tm, 1), 0)
    a1 = pltpu.roll(a, 1, 0)
    a2 = pltpu.roll(a, 2, 0)
    if T >= tm:
        tiles_per_batch = T // tm

        @pl.when(i % tiles_per_batch == 0)
        def _():
            carry_ref[j] = p1_ref[...]

        prev = carry_ref[j]
        older, newer = prev[SUBLANES - 2:SUBLANES - 1], prev[SUBLANES - 1:SUBLANES]
        a1 = jnp.where(row == 0, newer, a1)
        a2 = jnp.where(row == 0, older, jnp.where(row == 1, newer, a2))
        carry_ref[j] = a[tm - SUBLANES:tm]
        a_ref[...] = a[tm - SUBLANES:tm]
    else:
        tmod = row % T
        a1 = jnp.where(tmod == 0, p1_ref[...], a1)
        a2 = jnp.where(tmod < 2, p2_ref[...], a2)
        a_ref[...] = a
    wc = wc_ref[...]
    c = bc_ref[...] + a2 * wc[0:1] + a1 * wc[1:2] + a * wc[2:3]
    y_ref[...] = (_gelu(c) * b).astype(BF16)


def _ffn_in(x, g, w, wconv, bconv, buf, B, T):
    M, K = x.shape
    F = w.shape[1] // 2
    tn = 512
    nj = F // tn
    assert F % tn == 0
    if buf is None:
        buf = jnp.zeros((B, CONV_W - 1, F), F32)
    wc = jnp.concatenate([wconv, jnp.zeros((SUBLANES - CONV_W, F), F32)], axis=0)
    if T >= 512:
        tm = 512
        assert T % tm == 0
        tpb = T // tm
        p1 = jnp.concatenate([jnp.zeros((B, SUBLANES - 2, F), F32), buf], axis=1)
        p2 = p1
        p_specs = [pl.BlockSpec((None, SUBLANES, tn), lambda i, j: (i // tpb, 0, j))] * 2
        a_shape = jax.ShapeDtypeStruct((M // tm, SUBLANES, F), F32)
        a_spec = pl.BlockSpec((None, SUBLANES, tn), lambda i, j: (i, 0, j))
    else:
        tm = M
        assert T >= 2 and M % T == 0
        z = jnp.zeros((B, 1, F), F32)
        p1 = jnp.concatenate([buf[:, 1:2]] + [z] * (T - 1), axis=1).reshape(M, F)
        p2 = jnp.concatenate([buf[:, 0:1], buf[:, 1:2]] + [z] * (T - 2), axis=1).reshape(M, F)
        p_specs = [pl.BlockSpec((tm, tn), lambda i, j: (i, j))] * 2
        a_shape = jax.ShapeDtypeStruct((M, F), F32)
        a_spec = pl.BlockSpec((tm, tn), lambda i, j: (i, j))
    y, a = pl.pallas_call(
        functools.partial(_ffn_in_kernel, T=T, rows=min(tm, 128)),
        grid=(M // tm, nj),
        in_specs=[pl.BlockSpec((tm, K), lambda i, j: (i, 0)),
                  pl.BlockSpec((1, K), lambda i, j: (0, 0)),
                  pl.BlockSpec((K, tn), lambda i, j: (0, j)),
                  pl.BlockSpec((K, tn), lambda i, j: (0, j + nj)),
                  pl.BlockSpec((SUBLANES, tn), lambda i, j: (0, j)),
                  pl.BlockSpec((1, tn), lambda i, j: (0, j))] + p_specs,
        out_specs=[pl.BlockSpec((tm, tn), lambda i, j: (i, j)), a_spec],
        out_shape=[jax.ShapeDtypeStruct((M, F), BF16), a_shape],
        scratch_shapes=[pltpu.VMEM((tm, K), BF16), pltpu.VMEM((nj, SUBLANES, tn), F32)],
        compiler_params=_cparams(("arbitrary", "arbitrary")),
    )(x, g.reshape(1, K), w, w, wc, bconv.reshape(1, F), p1, p2)
    if T >= 512:
        new_buf = a.reshape(B, T // tm, SUBLANES, F)[:, -1, SUBLANES - (CONV_W - 1):]
    else:
        new_buf = a.reshape(B, T, F)
    return y, new_buf


def _cumsum_rows(x):
    n = x.shape[0]
    row = lax.broadcasted_iota(jnp.int32, (n, 1), 0)
    d = 1
    while d < n:
        x = x + jnp.where(row >= d, pltpu.roll(x, d, 0), 0.0)
        d *= 2
    return x


def _hgrn_kernel(uq_ref, uf_ref, uv_ref, ug_ref, lbl_ref, gn_ref, s0_ref, o_ref, sfin_ref,
                 st_ref, *, C, CP, nchunk):
    t = pl.program_id(2)

    @pl.when(t == 0)
    def _():
        st_ref[...] = s0_ref[...].T

    lbl = lbl_ref[...]
    e = jnp.exp(lbl - jnp.max(lbl, axis=0, keepdims=True))
    lb = e[0:1] / jnp.sum(e, axis=0, keepdims=True)
    gn = gn_ref[...]
    row = lax.broadcasted_iota(jnp.int32, (CP, 1), 0)
    nsub = CP // SUBLANES

    def chunk(c, carry):
        r0 = pl.multiple_of(c * CP, CP)
        uq = uq_ref[pl.ds(r0, CP), :]
        q = uq * _sigmoid(uq)
        fg = lb + (1.0 - lb) * _sigmoid(uf_ref[pl.ds(r0, CP), :])
        kk = 1.0 - fg
        lf = jnp.log(fg)
        if C < CP:
            kk = jnp.where(row < C, kk, 0.0)
            lf = jnp.where(row < C, lf, 0.0)
        v = uv_ref[pl.ds(r0, CP), :]
        b = _cumsum_rows(lf)
        st = st_ref[...]
        o = _dot_nt(_pad_rows(q * jnp.exp(b)).astype(BF16), st.astype(BF16))
        o_parts = [o[i * SUBLANES:(i + 1) * SUBLANES] for i in range(nsub)]
        for s in range(C):
            bs, ks, vs = b[s:s + 1], kk[s:s + 1], v[s:s + 1]
            for i in range(s // SUBLANES, nsub):
                sl = slice(i * SUBLANES, (i + 1) * SUBLANES)
                d = jnp.exp(jnp.minimum(b[sl] - bs, 0.0))
                a = jnp.sum(q[sl] * ks * d, axis=-1, keepdims=True)
                a = jnp.where(row[sl] >= s, a, 0.0)
                o_parts[i] = o_parts[i] + a * vs
        o = jnp.concatenate(o_parts, axis=0)
        b_last = b[CP - 1:CP]
        kd = kk * jnp.exp(b_last - b)
        st_ref[...] = st * jnp.exp(b_last) + _dot_tn(_pad_rows(v).astype(BF16), _pad_rows(kd).astype(BF16))
        og = _sigmoid(ug_ref[pl.ds(r0, CP), :])
        o_ref[pl.ds(r0, CP), :] = _rms(o, gn) * og
        return carry

    lax.fori_loop(0, nchunk, chunk, 0)

    @pl.when(t == pl.num_programs(2) - 1)
    def _():
        sfin_ref[...] = st_ref[...].T


def _hgrn(u, lb_logits, gnorm, s0, B, T, H, t_valid):
    C = math.gcd(t_valid, HGRN_CHUNK)
    assert C == t_valid or T == t_valid
    CP = max(C, SUBLANES)
    tt = _pick(T, (512, 256, 128, 64, 32, 16, 8))
    nchunk = tt // CP
    if s0 is None:
        s0 = jnp.zeros((B, H, HEAD_DIM, HEAD_DIM), F32)
    W = H * HEAD_DIM

    def col(part):
        return pl.BlockSpec((None, tt, HEAD_DIM), lambda b, h, t: (b, t, part * H + h))

    return pl.pallas_call(
        functools.partial(_hgrn_kernel, C=C, CP=CP, nchunk=nchunk),
        grid=(B, H, T // tt),
        in_specs=[col(0), col(1), col(2), col(3),
                  pl.BlockSpec((lb_logits.shape[0], HEAD_DIM), lambda b, h, t: (0, h)),
                  pl.BlockSpec((1, HEAD_DIM), lambda b, h, t: (0, h)),
                  pl.BlockSpec((None, None, HEAD_DIM, HEAD_DIM), lambda b, h, t: (b, h, 0, 0))],
        out_specs=[pl.BlockSpec((None, tt, HEAD_DIM), lambda b, h, t: (b, t, h)),
                   pl.BlockSpec((None, None, HEAD_DIM, HEAD_DIM), lambda b, h, t: (b, h, 0, 0))],
        out_shape=[jax.ShapeDtypeStruct((B, T, W), F32),
                   jax.ShapeDtypeStruct((B, H, HEAD_DIM, HEAD_DIM), F32)],
        scratch_shapes=[pltpu.VMEM((HEAD_DIM, HEAD_DIM), F32)],
        compiler_params=_cparams(("parallel", "parallel", "arbitrary")),
    )(u, u, u, u, lb_logits, gnorm.reshape(1, W), s0)


def _memattn_kernel(q_ref, kv_ref, o_ref):
    for h in range(MEM_HEADS):
        sl = slice(h * HEAD_DIM, (h + 1) * HEAD_DIM)
        q = _pad_rows(q_ref[:, sl]).astype(BF16)
        k = kv_ref[:, sl].astype(BF16)
        v = kv_ref[:, MEM_WIDTH + h * HEAD_DIM:MEM_WIDTH + (h + 1) * HEAD_DIM].astype(BF16)
        s = _dot_nt(q, k) * SCALE
        p = jnp.exp(s - jnp.max(s, axis=-1, keepdims=True))
        l = jnp.sum(p, axis=-1, keepdims=True)
        o_ref[:, sl] = (_dot(p.astype(BF16), v) / l)[:q_ref.shape[0]]


def _memattn(u, qblock, mkv, B, T):
    tq = _pick(T, (512, 256, 128, 64, 32, 16, 8))
    ml = mkv.shape[1]
    return pl.pallas_call(
        _memattn_kernel,
        grid=(B, T // tq),
        in_specs=[pl.BlockSpec((None, tq, MEM_WIDTH), lambda b, t: (b, t, qblock)),
                  pl.BlockSpec((None, ml, 2 * MEM_WIDTH), lambda b, t: (b, 0, 0))],
        out_specs=pl.BlockSpec((None, tq, MEM_WIDTH), lambda b, t: (b, t, 0)),
        out_shape=jax.ShapeDtypeStruct((B, T, MEM_WIDTH), F32),
        compiler_params=_cparams(("parallel", "parallel")),
    )(u, mkv)


def _cmp_bias_kernel(pe_ref, wc_ref, o_ref):
    o_ref[...] = _dot(pe_ref[...].astype(BF16), wc_ref[...])


def _cmp_bias(pe2, wc):
    return pl.pallas_call(
        _cmp_bias_kernel,
        grid=(2,),
        in_specs=[pl.BlockSpec((None, 16, pe2.shape[2]), lambda r: (r, 0, 0)),
                  pl.BlockSpec((None,) + wc.shape[1:], lambda r: (r, 0, 0))],
        out_specs=pl.BlockSpec((None, 16, wc.shape[2]), lambda r: (r, 0, 0)),
        out_shape=jax.ShapeDtypeStruct((2, 16, wc.shape[2]), F32),
    )(pe2, wc)


def _cmp_finish(part, cb, w2):
    nch = part.shape[0]
    bias = cb[0:1, 0:HEAD_DIM] + cb[1:2, HEAD_DIM:2 * HEAD_DIM]
    pre = bias + part[:, 0:HEAD_DIM] + pltpu.roll(part[:, HEAD_DIM:2 * HEAD_DIM], nch - 1, 0)
    return _dot(_gelu(pre).astype(BF16), w2)


def _cmp_prompt_kernel(kv0_ref, kv1_ref, wc_ref, w2_ref, cb_ref, o_ref, x_ref, *, nch):
    for g, kv_ref in enumerate((kv0_ref, kv1_ref)):
        for s in range(CMP_STRIDE):
            x = kv_ref[pl.ds(s, nch, stride=CMP_STRIDE), :]
            x_ref[g * nch:(g + 1) * nch, s * HEAD_DIM:(s + 1) * HEAD_DIM] = x.astype(BF16)
    part = _dot(x_ref[...], wc_ref[...])
    for g in range(G_B):
        o_ref[g] = _cmp_finish(part[g * nch:(g + 1) * nch], cb_ref[...], w2_ref[...])


def _cmp_prompt(kv, wc, w2, cb, B, T):
    nch = T // CMP_STRIDE
    assert nch % 16 == 0
    return pl.pallas_call(
        functools.partial(_cmp_prompt_kernel, nch=nch),
        grid=(B, 2),
        in_specs=[pl.BlockSpec((T, HEAD_DIM), lambda b, r: (b, G_B * r)),
                  pl.BlockSpec((T, HEAD_DIM), lambda b, r: (b, G_B * r + 1)),
                  pl.BlockSpec((None,) + wc.shape[1:], lambda b, r: (r, 0, 0)),
                  pl.BlockSpec((None, HEAD_DIM, HEAD_DIM), lambda b, r: (r, 0, 0)),
                  pl.BlockSpec((None, 16, wc.shape[2]), lambda b, r: (r, 0, 0))],
        out_specs=pl.BlockSpec((None, None, G_B, nch, HEAD_DIM), lambda b, r: (b, r, 0, 0, 0)),
        out_shape=jax.ShapeDtypeStruct((B, 2, G_B, nch, HEAD_DIM), F32),
        scratch_shapes=[pltpu.VMEM((G_B * nch, CMP_STRIDE * HEAD_DIM), BF16)],
        compiler_params=_cparams(("parallel", "parallel")),
    )(kv, kv, wc, w2, cb)


def _stack_heads(q, hpg):
    return jnp.concatenate([q[:, r * HEAD_DIM:(r + 1) * HEAD_DIM] for r in range(hpg)],
                           axis=0).astype(BF16)


def _softmax_pieces(pieces, hpg, tq, slopes):
    outs, probs = [], [[] for _ in pieces]
    for r in range(hpg):
        ss = []
        for (s, dist, mask, _) in pieces:
            sr = s[r * tq:(r + 1) * tq] - slopes[r] * dist
            ss.append(jnp.where(mask, sr, NEG))
        m = ss[0].max(axis=-1, keepdims=True)
        for sr in ss[1:]:
            m = jnp.maximum(m, sr.max(axis=-1, keepdims=True))
        es = [jnp.where(mask, jnp.exp(sr - m), 0.0) for sr, (_, _, mask, _) in zip(ss, pieces)]
        l = es[0].sum(axis=-1, keepdims=True)
        for ee in es[1:]:
            l = l + ee.sum(axis=-1, keepdims=True)
        inv = 1.0 / jnp.maximum(l, 1e-30)
        for i, ee in enumerate(es):
            probs[i].append(ee * inv)
    probs = [jnp.concatenate(p, axis=0) for p in probs]
    o = _dot(probs[0].astype(BF16), pieces[0][3])
    for p, piece in zip(probs[1:], pieces[1:]):
        o = o + _dot(p.astype(BF16), piece[3])
    return probs, o


def _sel_score_mask(imp, msel, qpos, nsel, ns):
    hi = imp.astype(BF16).astype(F32)
    mid = (imp - hi).astype(BF16).astype(F32)
    lo = imp - hi - mid
    p = _dot_fewrows(hi, msel) + _dot_fewrows(mid, msel) + _dot_fewrows(lo, msel)
    nsp = p.shape[1]
    j = lax.broadcasted_iota(jnp.int32, (1, nsp), 1)
    cur = qpos // SEL_BLOCK
    valid = (j <= cur) & (j < ns)
    forced = (j == 0) | (j == cur) | (j == cur - 1)
    score = jnp.where(valid, p + jnp.where(forced, FORCE_BONUS, 0.0), NEG)
    cnt = jnp.zeros(score.shape, F32)
    for i in range(ns):
        col = score[:, i:i + 1]
        beats = (col > score) | ((col == score) & (i < j))
        cnt = cnt + jnp.where(beats, 1.0, 0.0)
    return jnp.where((cnt < nsel) & valid, 1.0, 0.0)


def _nsa_prompt_kernel(sl_ref, q_ref, gt_ref, kc_ref, vc_ref, ks_ref, vs_ref, kw_ref, vw_ref,
                       msel_ref, e_ref, o_ref, *, tq, tk, hpg, nsel, ns, T):
    g = pl.program_id(1)
    qi = pl.program_id(2)
    q0 = qi * tq
    slopes = [sl_ref[g * hpg + r] for r in range(hpg)]
    qpos = q0 + lax.broadcasted_iota(jnp.int32, (tq, 1), 0)
    Q = _stack_heads(q_ref[...], hpg)

    ncp = kc_ref.shape[0]
    end = lax.broadcasted_iota(jnp.int32, (1, ncp), 1) * CMP_STRIDE + (CMP_BLOCK - 1)
    dist = qpos - end
    sc = _dot_nt(Q, kc_ref[...].astype(BF16)) * SCALE
    (pc,), o_cmp = _softmax_pieces([(sc, dist.astype(F32), dist >= 0, vc_ref[...].astype(BF16))],
                                   hpg, tq, slopes)
    imp = pc[0:tq]
    for r in range(1, hpg):
        imp = imp + pc[r * tq:(r + 1) * tq]
    selm = _sel_score_mask(imp, msel_ref[...], qpos, nsel, ns).astype(BF16)

    def kt_body(kt, carry):
        ms, ls, accs = carry
        k0 = pl.multiple_of(kt * tk, tk)
        kb = ks_ref[pl.ds(k0, tk), :].astype(BF16)
        vb = vs_ref[pl.ds(k0, tk), :].astype(BF16)
        s = _dot_nt(Q, kb) * SCALE
        picked = _dot(selm, e_ref[kt]) > 0.5
        kpos = k0 + lax.broadcasted_iota(jnp.int32, (1, tk), 1)
        d = qpos - kpos
        mask = picked & (d >= 0)
        df = d.astype(F32)
        new_ms, new_ls, ps, alphas = [], [], [], []
        for r in range(hpg):
            sr = jnp.where(mask, s[r * tq:(r + 1) * tq] - slopes[r] * df, NEG)
            m_new = jnp.maximum(ms[r], sr.max(axis=-1, keepdims=True))
            alpha = jnp.exp(ms[r] - m_new)
            p = jnp.where(mask, jnp.exp(sr - m_new), 0.0)
            new_ms.append(m_new)
            new_ls.append(alpha * ls[r] + p.sum(axis=-1, keepdims=True))
            ps.append(p)
            alphas.append(jnp.broadcast_to(alpha, (tq, HEAD_DIM)))
        pv = _dot(jnp.concatenate(ps, axis=0).astype(BF16), vb)
        accs = jnp.concatenate(alphas, axis=0) * accs + pv
        return tuple(new_ms), tuple(new_ls), accs

    init = (tuple(jnp.full((tq, 1), NEG, F32) for _ in range(hpg)),
            tuple(jnp.zeros((tq, 1), F32) for _ in range(hpg)),
            jnp.zeros((hpg * tq, HEAD_DIM), F32))
    nkt = (q0 + tq + tk - 1) // tk
    ms, ls, accs = lax.fori_loop(0, nkt, kt_body, init)
    inv = jnp.concatenate([jnp.broadcast_to(1.0 / jnp.maximum(l, 1e-30), (tq, HEAD_DIM)) for l in ls],
                          axis=0)
    o_sel = accs * inv

    wl = WINDOW + tq
    kstart = pl.multiple_of(jnp.maximum(q0 - WINDOW, 0), tq)
    kw = kw_ref[pl.ds(kstart, wl), :].astype(BF16)
    vw = vw_ref[pl.ds(kstart, wl), :].astype(BF16)
    sw = _dot_nt(Q, kw) * SCALE
    dw = qpos - (kstart + lax.broadcasted_iota(jnp.int32, (1, wl), 1))
    _, o_win = _softmax_pieces([(sw, dw.astype(F32), (dw >= 0) & (dw < WINDOW), vw)], hpg, tq, slopes)

    gt = _sigmoid(gt_ref[...])
    for r in range(hpg):
        rows = slice(r * tq, (r + 1) * tq)
        o_ref[:, r * HEAD_DIM:(r + 1) * HEAD_DIM] = (
            gt[:, 3 * r:3 * r + 1] * o_cmp[rows] + gt[:, 3 * r + 1:3 * r + 2] * o_sel[rows]
            + gt[:, 3 * r + 2:3 * r + 3] * o_win[rows])


def _alibi_slopes(h_b):
    h = np.arange(1, h_b + 1, dtype=np.float32)
    return jnp.asarray(np.exp2(-8.0 * h / h_b).astype(np.float32))


def _msel_matrix(ncp, nsp, nc, ns):
    i = np.arange(ncp)[:, None]
    j = np.arange(nsp)[None, :]
    r = SEL_BLOCK // CMP_STRIDE
    back = CMP_BLOCK // CMP_STRIDE - 1
    m = (i >= r * j - back) & (i <= r * j + r - 1) & (i < nc) & (j < ns)
    return jnp.asarray(m.astype(np.float32), dtype=BF16)


def _expand_matrix(nsteps, nsp, tk):
    s = np.arange(nsteps)[:, None, None]
    j = np.arange(nsp)[None, :, None]
    key = np.arange(tk)[None, None, :]
    return jnp.asarray(((s * tk + key) // SEL_BLOCK == j).astype(np.float32), dtype=BF16)


def _nsa_prompt(u, kv, kcv, B, T, hpg):
    tq, tk = 128, 512
    assert T % tk == 0 and T >= WINDOW + tq
    ncp = T // CMP_STRIDE
    nc = ncp - CMP_BLOCK // CMP_STRIDE + 1
    ns = -(-T // SEL_BLOCK)
    nsp = 128
    assert ns <= nsp
    nsel = min(N_SEL, ns)
    nqt = T // tq
    qw = hpg * HEAD_DIM
    gate_block0 = (G_B * qw + MEM_WIDTH) // HEAD_DIM
    msel = _msel_matrix(ncp, nsp, nc, ns)
    e3 = _expand_matrix(T // tk, nsp, tk)

    def kvcol(c):
        return pl.BlockSpec((T, HEAD_DIM), lambda b, g, t: (b, c + g))

    def cmpspec(r):
        return pl.BlockSpec((None, None, None, ncp, HEAD_DIM), lambda b, g, t: (b, r, g, 0, 0))

    return pl.pallas_call(
        functools.partial(_nsa_prompt_kernel, tq=tq, tk=tk, hpg=hpg, nsel=nsel, ns=ns, T=T),
        grid=(B, G_B, nqt),
        in_specs=[pl.BlockSpec(memory_space=pltpu.SMEM),
                  pl.BlockSpec((tq, qw), lambda b, g, t: (b * nqt + t, g)),
                  pl.BlockSpec((tq, HEAD_DIM), lambda b, g, t: (b * nqt + t, gate_block0 + g)),
                  cmpspec(0), cmpspec(1), kvcol(4), kvcol(6), kvcol(8), kvcol(10),
                  pl.BlockSpec(msel.shape, lambda b, g, t: (0, 0)),
                  pl.BlockSpec(e3.shape, lambda b, g, t: (0, 0, 0))],
        out_specs=pl.BlockSpec((tq, qw), lambda b, g, t: (b * nqt + t, g)),
        out_shape=jax.ShapeDtypeStruct((B * T, G_B * qw), F32),
        compiler_params=_cparams(("parallel", "parallel", "arbitrary")),
    )(_alibi_slopes(G_B * hpg), u, u, kcv, kcv, kv, kv, kv, kv, msel, e3)


CMP_PAGES = 16
SEL_PAGES = 8


def _cmp_pages_kernel(pt_ref, *refs, npage):
    page_refs = refs[:npage]
    wc_ref, o_ref, x_ref, stage_ref = refs[npage:]
    cpp = PAGE_SIZE // CMP_STRIDE
    rows = npage * cpp
    for rt in range(2):
        for g in range(G_B):
            c0 = (rt * G_B + g) * HEAD_DIM
            for p in range(npage):
                stage_ref[p * PAGE_SIZE:(p + 1) * PAGE_SIZE, :] = page_refs[p][:, c0:c0 + HEAD_DIM]
            for s in range(CMP_STRIDE):
                x_ref[rt, g * rows:(g + 1) * rows, s * HEAD_DIM:(s + 1) * HEAD_DIM] = (
                    stage_ref[pl.ds(s, rows, stride=CMP_STRIDE), :])
    for rt in range(2):
        part = _dot(x_ref[rt].astype(BF16), wc_ref[rt])
        for g in range(G_B):
            o_ref[rt, g] = part[g * rows:(g + 1) * rows]


def _cmp_pages(cache, page_table, wc):
    Bs, npg = page_table.shape
    npage = CMP_PAGES
    assert npg % npage == 0
    cpp = PAGE_SIZE // CMP_STRIDE
    rows = npage * cpp
    nch = npg * cpp
    half = 2 * G_B * HEAD_DIM

    def page_spec(p):
        return pl.BlockSpec((None, PAGE_SIZE, half), lambda b, s, pt: (pt[b, s * npage + p], 0, 0))

    grid_spec = pltpu.PrefetchScalarGridSpec(
        num_scalar_prefetch=1,
        grid=(Bs, npg // npage),
        in_specs=[page_spec(p) for p in range(npage)]
        + [pl.BlockSpec(wc.shape, lambda b, s, pt: (0, 0, 0))],
        out_specs=pl.BlockSpec((None, 2, G_B, rows, wc.shape[2]), lambda b, s, pt: (b, 0, 0, s, 0)),
        scratch_shapes=[pltpu.VMEM((2, G_B * rows, CMP_STRIDE * HEAD_DIM), F32),
                        pltpu.VMEM((npage * PAGE_SIZE, HEAD_DIM), F32)],
    )
    return pl.pallas_call(
        functools.partial(_cmp_pages_kernel, npage=npage),
        grid_spec=grid_spec,
        out_shape=jax.ShapeDtypeStruct((Bs, 2, G_B, nch, wc.shape[2]), F32),
        compiler_params=_cparams(("parallel", "arbitrary")),
    )(page_table, *([cache] * npage), wc)


def _nsa_sample_cmp_kernel(sl_ref, part_ref, cb_ref, w2_ref, q_ref, msel_ref, ocmp_ref, selm_ref,
                           *, tp, hpg, nsel, ns, past):
    g = pl.program_id(1)
    slopes = [sl_ref[g * hpg + r] for r in range(hpg)]
    kc = _cmp_finish(part_ref[0], cb_ref[0], w2_ref[0]).astype(BF16)
    vc = _cmp_finish(part_ref[1], cb_ref[1], w2_ref[1]).astype(BF16)
    nch = kc.shape[0]
    qpos = past + lax.broadcasted_iota(jnp.int32, (tp, 1), 0)
    Q = _stack_heads(q_ref[...], hpg)
    idx = lax.broadcasted_iota(jnp.int32, (1, nch), 1)
    dist = qpos - (idx * CMP_STRIDE + (CMP_BLOCK - 1))
    mask = (dist >= 0) & (idx < nch - 1)
    sc = _dot_nt(Q, kc) * SCALE
    (pc,), o_cmp = _softmax_pieces([(sc, dist.astype(F32), mask, vc)], hpg, tp, slopes)
    imp = pc[0:tp]
    for r in range(1, hpg):
        imp = imp + pc[r * tp:(r + 1) * tp]
    ocmp_ref[...] = o_cmp
    selm_ref[...] = _sel_score_mask(imp, msel_ref[...], qpos, nsel, ns)


def _nsa_sample_cmp(u, part, cb, w2, Bs, tp, hpg, past):
    nch = part.shape[3]
    nc = nch - CMP_BLOCK // CMP_STRIDE + 1
    ns = past // SEL_BLOCK + 1
    nsp = -(-ns // 128) * 128
    nsel = min(N_SEL, ns)
    qw = hpg * HEAD_DIM
    msel = _msel_matrix(nch, nsp, nc, ns)
    return pl.pallas_call(
        functools.partial(_nsa_sample_cmp_kernel, tp=tp, hpg=hpg, nsel=nsel, ns=ns, past=past),
        grid=(Bs, G_B),
        in_specs=[pl.BlockSpec(memory_space=pltpu.SMEM),
                  pl.BlockSpec((None, 2, None, nch, part.shape[4]), lambda b, g: (b, 0, g, 0, 0)),
                  pl.BlockSpec(cb.shape, lambda b, g: (0, 0, 0)),
                  pl.BlockSpec(w2.shape, lambda b, g: (0, 0, 0)),
                  pl.BlockSpec((None, tp, qw), lambda b, g: (b, 0, g)),
                  pl.BlockSpec(msel.shape, lambda b, g: (0, 0))],
        out_specs=[pl.BlockSpec((None, None, hpg * tp, HEAD_DIM), lambda b, g: (b, g, 0, 0)),
                   pl.BlockSpec((None, None, tp, nsp), lambda b, g: (b, g, 0, 0))],
        out_shape=[jax.ShapeDtypeStruct((Bs, G_B, hpg * tp, HEAD_DIM), F32),
                   jax.ShapeDtypeStruct((Bs, G_B, tp, nsp), F32)],
        compiler_params=_cparams(("parallel", "parallel")),
    )(_alibi_slopes(G_B * hpg), part, cb, w2, u, msel)


def _nsa_sample_kernel(pt_ref, sl_ref, *refs, npage, tp, hpg, past):
    page_refs = refs[:npage]
    q_ref, gt_ref, selm_ref, ocmp_ref, kvn_ref, cwin_ref, e_ref, o_ref, m_ref, l_ref, acc_ref = refs[npage:]
    s_id = pl.program_id(1)
    nsteps = pl.num_programs(1)
    tk = npage * PAGE_SIZE
    rows = hpg * tp
    qpos = past + lax.broadcasted_iota(jnp.int32, (tp, 1), 0)

    @pl.when(s_id == 0)
    def _():
        m_ref[...] = jnp.full(m_ref.shape, NEG, F32)
        l_ref[...] = jnp.zeros(l_ref.shape, F32)
        acc_ref[...] = jnp.zeros(acc_ref.shape, F32)

    kpos = s_id * tk + lax.broadcasted_iota(jnp.int32, (1, tk), 1)
    d = qpos - kpos
    df = d.astype(F32)
    for g in range(G_B):
        slopes = [sl_ref[g * hpg + r] for r in range(hpg)]
        Q = _stack_heads(q_ref[:, g * hpg * HEAD_DIM:(g + 1) * hpg * HEAD_DIM], hpg)
        kc0 = (2 * G_B + g) * HEAD_DIM - 2 * G_B * HEAD_DIM
        vc0 = kc0 + G_B * HEAD_DIM
        kb = jnp.concatenate([page_refs[p][:, kc0:kc0 + HEAD_DIM] for p in range(npage)], axis=0).astype(BF16)
        vb = jnp.concatenate([page_refs[p][:, vc0:vc0 + HEAD_DIM] for p in range(npage)], axis=0).astype(BF16)
        s = _dot_nt(Q, kb) * SCALE
        mask = (_dot_fewrows(selm_ref[g], e_ref[...]) > 0.5) & (d >= 0)
        ps, alphas = [], []
        for r in range(hpg):
            hr = slice(r * tp, (r + 1) * tp)
            sr = jnp.where(mask, s[hr] - slopes[r] * df, NEG)
            m_old = m_ref[g, hr]
            m_new = jnp.maximum(m_old, sr.max(axis=-1, keepdims=True))
            alpha = jnp.exp(m_old - m_new)
            p = jnp.where(mask, jnp.exp(sr - m_new), 0.0)
            m_ref[g, hr] = m_new
            l_ref[g, hr] = alpha * l_ref[g, hr] + p.sum(axis=-1, keepdims=True)
            ps.append(p)
            alphas.append(jnp.broadcast_to(alpha, (tp, HEAD_DIM)))
        pv = _dot(jnp.concatenate(ps, axis=0).astype(BF16), vb)
        acc_ref[g] = jnp.concatenate(alphas, axis=0) * acc_ref[g] + pv

    @pl.when(s_id == nsteps - 1)
    def _():
        gt = _sigmoid(gt_ref[...])
        npad = HEAD_DIM
        kidx = lax.broadcasted_iota(jnp.int32, (1, npad), 1)
        dn = qpos - (past + kidx)
        new_ok = (kidx < tp) & (dn >= 0)
        zpad = jnp.zeros((npad - tp, HEAD_DIM), F32)
        nblk = past // SEL_BLOCK
        wpast = cwin_ref.shape[0]
        dwp = qpos - (past - wpast + lax.broadcasted_iota(jnp.int32, (1, wpast), 1))
        for g in range(G_B):
            slopes = [sl_ref[g * hpg + r] for r in range(hpg)]
            Q = _stack_heads(q_ref[:, g * hpg * HEAD_DIM:(g + 1) * hpg * HEAD_DIM], hpg)

            def newrows(c):
                blk = kvn_ref[:, (c * G_B + g) * HEAD_DIM:(c * G_B + g + 1) * HEAD_DIM]
                return jnp.concatenate([blk, zpad], axis=0).astype(BF16)

            kn, vn = newrows(2), newrows(3)
            sn = _dot_nt(Q, kn) * SCALE
            mask_n = new_ok & (selm_ref[g][:, nblk:nblk + 1] > 0.5)
            o_sel = []
            for r in range(hpg):
                hr = slice(r * tp, (r + 1) * tp)
                sr = jnp.where(mask_n, sn[hr] - slopes[r] * dn.astype(F32), NEG)
                m_old = m_ref[g, hr]
                m_new = jnp.maximum(m_old, sr.max(axis=-1, keepdims=True))
                alpha = jnp.exp(m_old - m_new)
                p = jnp.where(mask_n, jnp.exp(sr - m_new), 0.0)
                l = alpha * l_ref[g, hr] + p.sum(axis=-1, keepdims=True)
                acc = alpha * acc_ref[g, hr] + _dot_fewrows(p, vn)
                o_sel.append(acc / jnp.maximum(l, 1e-30))
            kwp = cwin_ref[:, g * HEAD_DIM:(g + 1) * HEAD_DIM].astype(BF16)
            vwp = cwin_ref[:, (G_B + g) * HEAD_DIM:(G_B + g + 1) * HEAD_DIM].astype(BF16)
            kwn, vwn = newrows(4), newrows(5)
            pieces = [(_dot_nt(Q, kwp) * SCALE, dwp.astype(F32), (dwp >= 0) & (dwp < WINDOW), vwp),
                      (_dot_nt(Q, kwn) * SCALE, dn.astype(F32), new_ok & (dn < WINDOW), vwn)]
            _, o_win = _softmax_pieces(pieces, hpg, tp, slopes)
            o_cmp = ocmp_ref[g]
            for r in range(hpg):
                hr = slice(r * tp, (r + 1) * tp)
                c = 3 * (g * hpg + r)
                col = (g * hpg + r) * HEAD_DIM
                o_ref[:, col:col + HEAD_DIM] = (gt[:, c:c + 1] * o_cmp[hr] + gt[:, c + 1:c + 2] * o_sel[r]
                                                + gt[:, c + 2:c + 3] * o_win[hr])


def _nsa_sample(u, cache, page_table, selm, o_cmp, kvn, cwin, Bs, tp, hpg, past):
    npg = page_table.shape[1]
    npage = SEL_PAGES
    assert npg % npage == 0
    nsteps = npg // npage
    tk = npage * PAGE_SIZE
    nsp = selm.shape[3]
    half = 2 * G_B * HEAD_DIM
    qw = G_B * hpg * HEAD_DIM
    e3 = _expand_matrix(nsteps, nsp, tk)
    gates = jnp.concatenate([u[:, :, qw + MEM_WIDTH + g * HEAD_DIM:qw + MEM_WIDTH + g * HEAD_DIM + 3 * hpg]
                             for g in range(G_B)]
                            + [jnp.zeros((Bs, tp, HEAD_DIM - 3 * hpg * G_B), F32)], axis=2)

    def page_spec(p):
        return pl.BlockSpec((None, PAGE_SIZE, half), lambda b, s, pt: (pt[b, s * npage + p], 0, 1))

    grid_spec = pltpu.PrefetchScalarGridSpec(
        num_scalar_prefetch=1,
        grid=(Bs, nsteps),
        in_specs=[pl.BlockSpec(memory_space=pltpu.SMEM)]
        + [page_spec(p) for p in range(npage)]
        + [pl.BlockSpec((None, tp, qw), lambda b, s, pt: (b, 0, 0)),
           pl.BlockSpec((None, tp, HEAD_DIM), lambda b, s, pt: (b, 0, 0)),
           pl.BlockSpec((None, G_B, tp, nsp), lambda b, s, pt: (b, 0, 0, 0)),
           pl.BlockSpec((None, G_B, hpg * tp, HEAD_DIM), lambda b, s, pt: (b, 0, 0, 0)),
           pl.BlockSpec((None, tp, kvn.shape[2]), lambda b, s, pt: (b, 0, 0)),
           pl.BlockSpec((None,) + cwin.shape[1:], lambda b, s, pt: (b, 0, 0)),
           pl.BlockSpec((None, nsp, tk), lambda b, s, pt: (s, 0, 0))],
        out_specs=pl.BlockSpec((None, tp, qw), lambda b, s, pt: (b, 0, 0)),
        scratch_shapes=[pltpu.VMEM((G_B, hpg * tp, 1), F32), pltpu.VMEM((G_B, hpg * tp, 1), F32),
                        pltpu.VMEM((G_B, hpg * tp, HEAD_DIM), F32)],
    )
    return pl.pallas_call(
        functools.partial(_nsa_sample_kernel, npage=npage, tp=tp, hpg=hpg, past=past),
        grid_spec=grid_spec,
        out_shape=jax.ShapeDtypeStruct((Bs, tp, qw), F32),
        compiler_params=_cparams(("parallel", "arbitrary")),
    )(page_table, _alibi_slopes(G_B * hpg), *([cache] * npage), u, gates, selm, o_cmp, kvn, cwin, e3)


def _prep_weights(w_in_a, w_in_b, w_o, w_mem_kv, w_kv_b, w_cmp1, w_cmp2, w_ffn_in, w_ffn_out, cmp_pos, hpg):
    tokw = w_in_b.shape[2] - 3 * G_B * hpg - MEM_WIDTH
    D = w_in_b.shape[1]
    wb = w_in_b[0]
    gates = wb[:, tokw:tokw + 3 * G_B * hpg]
    gpad = jnp.zeros((D, HEAD_DIM - 3 * hpg), F32)
    wb2 = jnp.concatenate([wb[:, :tokw], wb[:, tokw + 3 * G_B * hpg:]]
                          + sum([[gates[:, g * 3 * hpg:(g + 1) * 3 * hpg], gpad] for g in range(G_B)], []),
                          axis=1)
    r = CMP_BLOCK // CMP_STRIDE
    flat = CMP_STRIDE * HEAD_DIM
    wc = w_cmp1.reshape(2, r, flat, HEAD_DIM).transpose(0, 2, 1, 3).reshape(2, flat, r * HEAD_DIM)
    pe2 = jnp.concatenate([cmp_pos.reshape(2, r, flat), jnp.zeros((2, 16 - r, flat), F32)], axis=1)
    return dict(w_in_a=w_in_a.astype(BF16), w_in_b=wb2.astype(BF16), w_o=w_o.astype(BF16),
                w_mem_kv=w_mem_kv.astype(BF16), w_kv_b=w_kv_b.astype(BF16), wc=wc.astype(BF16),
                w2=w_cmp2.astype(BF16), w_ffn_in=w_ffn_in.astype(BF16), w_ffn_out=w_ffn_out.astype(BF16),
                pe2=pe2)


def _layer_tail(h, mix, mo, l, prm, conv_buf, B, T):
    g = prm['norm_gains'][l]
    h = _mmres(mix, mo, prm['w_o'][l], h, g[1])
    y, new_buf = _ffn_in(h, g[2], prm['w_ffn_in'][l], prm['w_ffn_conv'][l], prm['b_ffn_conv'][l],
                         None if conv_buf is None else conv_buf[l], B, T)
    h = _mmres(y, None, prm['w_ffn_out'][l], h, g[3])
    return h, new_buf


def _trunk(x, mem_kv, hgrn_s0, conv_buf, prm, B, T, t_valid, nsa_fn):
    D = x.shape[-1]
    H = prm['hgrn_norm'].shape[1]
    h = x.reshape(B * T, D)
    u = _mm(h, prm['norm_gains'][0, 0], prm['w_in_a'][0]).reshape(B, T, -1)
    mix, s_fin = _hgrn(u, prm['lb_logits'], prm['hgrn_norm'][0], None if hgrn_s0 is None else hgrn_s0[0],
                       B, T, H, t_valid)
    mo = _memattn(u, 4 * H * HEAD_DIM // MEM_WIDTH, mem_kv[0], B, T)
    h, buf0 = _layer_tail(h, mix.reshape(B * T, -1), mo.reshape(B * T, -1), 0, prm, conv_buf, B, T)
    kv = _mm(h, prm['kv_norm'], prm['w_kv_b'])
    u = _mm(h, prm['norm_gains'][1, 0], prm['w_in_b'])
    mix = nsa_fn(u, kv)
    mo = _memattn(u.reshape(B, T, -1), H * HEAD_DIM // MEM_WIDTH, mem_kv[1], B, T)
    h, buf1 = _layer_tail(h, mix.reshape(B * T, -1), mo.reshape(B * T, -1), 1, prm, conv_buf, B, T)
    return h.reshape(B, T, D), s_fin, (buf0, buf1), kv


def kernel(x_prompt, x_sample, mem_prompt, state_hgrn, cache_conv, cache_mem, cache_kv, cache_win,
           page_table, norm_gains, w_in_a, lb_logits, hgrn_norm, w_in_b, w_o, w_mem_kv, kv_norm,
           w_kv_b, cmp_pos, w_cmp1, w_cmp2, w_ffn_in, w_ffn_conv, b_ffn_conv, w_ffn_out):
    Bp, Tp, D = x_prompt.shape
    Bs, Ts, _ = x_sample.shape
    depth = norm_gains.shape[0]
    assert depth == 2 and w_in_a.shape[0] == 1 and w_in_b.shape[0] == 1
    H = hgrn_norm.shape[1]
    hpg = H // G_B
    ml = mem_prompt.shape[1]
    n_rows = cache_kv.shape[2]

    prm = _prep_weights(w_in_a, w_in_b, w_o, w_mem_kv, w_kv_b, w_cmp1, w_cmp2, w_ffn_in, w_ffn_out,
                        cmp_pos, hpg)
    prm.update(norm_gains=norm_gains, lb_logits=lb_logits, hgrn_norm=hgrn_norm, kv_norm=kv_norm,
               w_ffn_conv=w_ffn_conv, b_ffn_conv=b_ffn_conv)
    cb = _cmp_bias(prm['pe2'], prm['wc'])

    memx = mem_prompt.reshape(Bp * ml, D)
    mem_kv_p = jnp.stack([_mm(memx, None, prm['w_mem_kv'][l], norm=False) for l in range(depth)])
    mem_kv_p = mem_kv_p.reshape(depth, Bp, ml, 2 * MEM_WIDTH)

    def nsa_p(u, kv):
        kcv = _cmp_prompt(kv, prm['wc'], prm['w2'], cb, Bp, Tp)
        return _nsa_prompt(u, kv, kcv, Bp, Tp, hpg)

    y_p, hgrn_p, conv_p, kv_p = _trunk(x_prompt, mem_kv_p, None, None, prm, Bp, Tp, Tp, nsa_p)

    tp = -(-Ts // SUBLANES) * SUBLANES
    past = page_table.shape[1] * PAGE_SIZE
    assert Ts < CMP_STRIDE and tp <= SEL_BLOCK and past >= WINDOW
    xs =jnp.concatenate([x_sample, jnp.zeros((Bs, tp - Ts, D), F32)], axis=1)
    conv_s0 = cache_conv
    cache2 = cache_kv.reshape(cache_kv.shape[0], PAGE_SIZE, -1)
    cwin = cache_win.reshape(Bs, cache_win.shape[1], -1)

    def nsa_s(u, kv):
        u3 = u.reshape(Bs, tp, -1)
        part = _cmp_pages(cache2, page_table, prm['wc'])
        o_cmp, selm = _nsa_sample_cmp(u3, part, cb, prm['w2'], Bs, tp, hpg, past)
        return _nsa_sample(u3, cache2, page_table, selm, o_cmp, kv.reshape(Bs, tp, -1), cwin,
                           Bs, tp, hpg, past)

    y_s, hgrn_s, conv_s, kv_s = _trunk(xs, cache_mem.reshape(depth, Bs, ml, -1), state_hgrn, conv_s0,
                                       prm, Bs, tp, Ts, nsa_s)

    nkv = n_rows * G_B * HEAD_DIM
    kv_p3 = kv_p.reshape(Bp, Tp, -1)
    kv_s3 = kv_s.reshape(Bs, tp, -1)[:, :Ts]
    wl = min(WINDOW, Tp)
    return (
        y_p,
        y_s[:, :Ts],
        hgrn_p[None],
        hgrn_s[None],
        jnp.stack(conv_p),
        jnp.stack([c[:, Ts - (CONV_W - 1):Ts] for c in conv_s]),
        mem_kv_p.reshape(depth, Bp, ml, 2, MEM_HEADS, HEAD_DIM),
        kv_p3[:, :, :nkv].reshape(Bp, Tp // PAGE_SIZE, PAGE_SIZE, n_rows, G_B, HEAD_DIM),
        kv_s3[:, :, :nkv].reshape(Bs, Ts, n_rows, G_B, HEAD_DIM),
        kv_p3[:, Tp - wl:, nkv:].reshape(Bp, wl, 2, G_B, HEAD_DIM),
        kv_s3[:, :, nkv:].reshape(Bs, Ts, 2, G_B, HEAD_DIM),
    )
```

```python
import functools
import math

import ml_dtypes
import numpy as np
import jax
import jax.numpy as jnp
from jax import lax
from jax.experimental import pallas as pl
from jax.experimental.pallas import tpu as pltpu

F32 = jnp.float32
BF16 = jnp.bfloat16

HEAD_DIM = 128
MEM_HEADS = 4
MEM_WIDTH = MEM_HEADS * HEAD_DIM
G_B = 2
HGRN_CHUNK = 32
CMP_BLOCK = 32
CMP_STRIDE = 16
SEL_BLOCK = 64
N_SEL = 16
WINDOW = 512
PAGE_SIZE = 128
FORCE_BONUS = 1e4
CONV_W = 3
EPS = 1e-6
NEG = -1e30
MASK_BF16 = -2.0 ** 100
POS_LANE = 120
SCALE = HEAD_DIM ** -0.5
SUBLANES = 8
HGRN_HEADS = 4
VMEM_LIMIT = 56 * 1024 * 1024


def _cparams(sem):
    return pltpu.CompilerParams(dimension_semantics=sem, vmem_limit_bytes=VMEM_LIMIT)


def _dot(a, b):
    return jnp.dot(a, b, preferred_element_type=F32)


def _dot_nt(a, b):
    return lax.dot_general(a, b, (((1,), (1,)), ((), ())), preferred_element_type=F32)


def _dot_tn(a, b):
    return lax.dot_general(a, b, (((0,), (0,)), ((), ())), preferred_element_type=F32)


BF16_ROWS = 16


def _pad_rows(x):
    n = x.shape[0]
    if n % BF16_ROWS == 0:
        return x
    return jnp.concatenate([x, jnp.zeros((BF16_ROWS - n % BF16_ROWS,) + x.shape[1:], x.dtype)], axis=0)


def _dot_fewrows(a, b):
    return _dot(_pad_rows(a).astype(BF16), b)[:a.shape[0]]


def _rms(x, g):
    return x * lax.rsqrt(jnp.mean(x * x, axis=-1, keepdims=True) + EPS) * g


def _sigmoid(x):
    return 0.5 * jnp.tanh(0.5 * x) + 0.5


def _gelu(x):
    return 0.5 * x * (1.0 + jnp.tanh(0.7978845608028654 * (x + 0.044715 * (x * x * x))))


def _pick(n, cands):
    for c in cands:
        if n % c == 0:
            return c
    raise ValueError(f"no tile for {n} in {cands}")


def _mm_kernel(x_ref, g_ref, w_ref, o_ref, xn_ref, *, norm, rows):
    tm = x_ref.shape[0]

    @pl.when(pl.program_id(1) == 0)
    def _():
        def body(c, carry):
            r = pl.multiple_of(c * rows, rows)
            x = x_ref[pl.ds(r, rows), :]
            if norm:
                x = _rms(x, g_ref[...])
            xn_ref[pl.ds(r, rows), :] = x.astype(BF16)
            return carry
        lax.fori_loop(0, tm // rows, body, 0)

    o_ref[...] = _dot(xn_ref[...], w_ref[...])


def _mm(x, g, w, *, norm=True):
    M, K = x.shape
    N = w.shape[1]
    tm = _pick(M, (1024, 512, 256, 128, 64, 32, 16))
    tn = _pick(N, (512, 768, 256, 128))
    rows = min(tm, 128)
    if g is None:
        g = jnp.ones((K,), F32)
    return pl.pallas_call(
        functools.partial(_mm_kernel, norm=norm, rows=rows),
        grid=(M // tm, N // tn),
        in_specs=[pl.BlockSpec((tm, K), lambda i, j: (i, 0)),
                  pl.BlockSpec((1, K), lambda i, j: (0, 0)),
                  pl.BlockSpec((K, tn), lambda i, j: (0, j))],
        out_specs=pl.BlockSpec((tm, tn), lambda i, j: (i, j)),
        out_shape=jax.ShapeDtypeStruct((M, N), F32),
        scratch_shapes=[pltpu.VMEM((tm, K), BF16)],
        compiler_params=_cparams(("parallel", "arbitrary")),
    )(x, g.reshape(1, K), w)


def _mmres_kernel(*refs, nk, two):
    if two:
        a1_ref, a2_ref, w_ref, h_ref, g_ref, o_ref = refs
        a = jnp.concatenate([a1_ref[...].astype(BF16), a2_ref[...].astype(BF16)], axis=1)
    else:
        a1_ref, w_ref, h_ref, g_ref, o_ref = refs
        a = a1_ref[...].astype(BF16)
    k = pl.program_id(1)
    part = _dot(a, w_ref[...])
    if nk == 1:
        o_ref[...] = h_ref[...] + _rms(part, g_ref[...])
    else:
        @pl.when(k == 0)
        def _():
            o_ref[...] = part

        @pl.when((k > 0) & (k < nk - 1))
        def _():
            o_ref[...] += part

        @pl.when(k == nk - 1)
        def _():
            o_ref[...] = h_ref[...] + _rms(o_ref[...] + part, g_ref[...])


def _mmres(a1, a2, w, h, g):
    M, N = h.shape
    K = w.shape[0]
    two = a2 is not None
    nk = 1 if two else 2
    tk = K // nk
    assert K % nk == 0 and tk % 128 == 0
    tm = _pick(M, (512, 256, 128, 64, 32, 16))
    if two:
        assert a1.shape[1] + a2.shape[1] == K
        in_specs = [pl.BlockSpec((tm, a1.shape[1]), lambda i, k: (i, 0)),
                    pl.BlockSpec((tm, a2.shape[1]), lambda i, k: (i, 0))]
        args = [a1, a2]
    else:
        in_specs = [pl.BlockSpec((tm, tk), lambda i, k: (i, k))]
        args = [a1]
    in_specs += [pl.BlockSpec((tk, N), lambda i, k: (k, 0)),
                 pl.BlockSpec((tm, N), lambda i, k: (i, 0)),
                 pl.BlockSpec((1, N), lambda i, k: (0, 0))]
    args += [w, h, g.reshape(1, N)]
    return pl.pallas_call(
        functools.partial(_mmres_kernel, nk=nk, two=two),
        grid=(M // tm, nk),
        in_specs=in_specs,
        out_specs=pl.BlockSpec((tm, N), lambda i, k: (i, 0)),
        out_shape=jax.ShapeDtypeStruct((M, N), F32),
        compiler_params=_cparams(("parallel", "arbitrary")),
    )(*args)


def _ffn_in_kernel(x_ref, g_ref, wa_ref, wb_ref, wc_ref, bc_ref, p1_ref, p2_ref,
                   y_ref, a_ref, xn_ref, carry_ref, *, T, rows):
    tm = x_ref.shape[0]
    i = pl.program_id(0)
    j = pl.program_id(1)

    @pl.when(j == 0)
    def _():
        def body(c, carry):
            r = pl.multiple_of(c * rows, rows)
            xn_ref[pl.ds(r, rows), :] = _rms(x_ref[pl.ds(r, rows), :], g_ref[...]).astype(BF16)
            return carry
        lax.fori_loop(0, tm // rows, body, 0)

    xn = xn_ref[...]
    a = _dot(xn, wa_ref[...])
    b = _dot(xn, wb_ref[...])
    row = lax.broadcasted_iota(jnp.int32, (tm, 1), 0)
    a1 = pltpu.roll(a, 1, 0)
    a2 = pltpu.roll(a, 2, 0)
    if T >= tm:
        tiles_per_batch = T // tm

        @pl.when(i % tiles_per_batch == 0)
        def _():
            carry_ref[j] = p1_ref[...]

        prev = carry_ref[j]
        older, newer = prev[SUBLANES - 2:SUBLANES - 1], prev[SUBLANES - 1:SUBLANES]
        a1 = jnp.where(row == 0, newer, a1)
        a2 = jnp.where(row == 0, older, jnp.where(row == 1, newer, a2))
        carry_ref[j] = a[tm - SUBLANES:tm]
        a_ref[...] = a[tm - SUBLANES:tm]
    else:
        tmod = row % T
        a1 = jnp.where(tmod == 0, p1_ref[...], a1)
        a2 = jnp.where(tmod < 2, p2_ref[...], a2)
        a_ref[...] = a
    wc = wc_ref[...]
    c = bc_ref[...] + a2 * wc[0:1] + a1 * wc[1:2] + a * wc[2:3]
    y_ref[...] = (_gelu(c) * b).astype(BF16)


def _ffn_in(x, g, w, wconv, bconv, buf, B, T):
    M, K = x.shape
    F = w.shape[1] // 2
    tn = 512
    nj = F // tn
    assert F % tn == 0
    if buf is None:
        buf = jnp.zeros((B, CONV_W - 1, F), F32)
    wc = jnp.concatenate([wconv, jnp.zeros((SUBLANES - CONV_W, F), F32)], axis=0)
    if T >= 512:
        tm = 512
        assert T % tm == 0
        tpb = T // tm
        p1 = jnp.concatenate([jnp.zeros((B, SUBLANES - 2, F), F32), buf], axis=1)
        p2 = p1
        p_specs = [pl.BlockSpec((None, SUBLANES, tn), lambda i, j: (i // tpb, 0, j))] * 2
        a_shape = jax.ShapeDtypeStruct((M // tm, SUBLANES, F), F32)
        a_spec = pl.BlockSpec((None, SUBLANES, tn), lambda i, j: (i, 0, j))
    else:
        tm = M
        assert T >= 2 and M % T == 0
        z = jnp.zeros((B, 1, F), F32)
        p1 = jnp.concatenate([buf[:, 1:2]] + [z] * (T - 1), axis=1).reshape(M, F)
        p2 = jnp.concatenate([buf[:, 0:1], buf[:, 1:2]] + [z] * (T - 2), axis=1).reshape(M, F)
        p_specs = [pl.BlockSpec((tm, tn), lambda i, j: (i, j))] * 2
        a_shape = jax.ShapeDtypeStruct((M, F), F32)
        a_spec = pl.BlockSpec((tm, tn), lambda i, j: (i, j))
    y, a = pl.pallas_call(
        functools.partial(_ffn_in_kernel, T=T, rows=min(tm, 128)),
        grid=(M // tm, nj),
        in_specs=[pl.BlockSpec((tm, K), lambda i, j: (i, 0)),
                  pl.BlockSpec((1, K), lambda i, j: (0, 0)),
                  pl.BlockSpec((K, tn), lambda i, j: (0, j)),
                  pl.BlockSpec((K, tn), lambda i, j: (0, j + nj)),
                  pl.BlockSpec((SUBLANES, tn), lambda i, j: (0, j)),
                  pl.BlockSpec((1, tn), lambda i, j: (0, j))] + p_specs,
        out_specs=[pl.BlockSpec((tm, tn), lambda i, j: (i, j)), a_spec],
        out_shape=[jax.ShapeDtypeStruct((M, F), BF16), a_shape],
        scratch_shapes=[pltpu.VMEM((tm, K), BF16), pltpu.VMEM((nj, SUBLANES, tn), F32)],
        compiler_params=_cparams(("arbitrary", "arbitrary")),
    )(x, g.reshape(1, K), w, w, wc, bconv.reshape(1, F), p1, p2)
    if T >= 512:
        new_buf = a.reshape(B, T // tm, SUBLANES, F)[:, -1, SUBLANES - (CONV_W - 1):]
    else:
        new_buf = a.reshape(B, T, F)
    return y, new_buf


def _cumsum_tile(x):
    row = lax.broadcasted_iota(jnp.int32, (SUBLANES, 1), 0)
    d = 1
    while d < SUBLANES:
        x = x + jnp.where(row >= d, pltpu.roll(x, d, 0), 0.0)
        d *= 2
    return x


def _hgrn_chunk(uq, uf, v, ug, lb, gn, st, C, CP):
    nsub = CP // SUBLANES
    row = lax.broadcasted_iota(jnp.int32, (CP, 1), 0)
    row8 = row[0:SUBLANES]
    q = uq * _sigmoid(uq)
    fg = lb + (1.0 - lb) * _sigmoid(uf)
    kk = 1.0 - fg
    lf = jnp.log(fg)
    if C < CP:
        kk = jnp.where(row < C, kk, 0.0)
        lf = jnp.where(row < C, lf, 0.0)
    b_tiles, ends, off = [], [], None
    for i in range(nsub):
        bi = _cumsum_tile(lf[i * SUBLANES:(i + 1) * SUBLANES])
        if off is not None:
            bi = bi + off
        off = bi[SUBLANES - 1:SUBLANES]
        b_tiles.append(bi)
        ends.append(off)
    b = jnp.concatenate(b_tiles, axis=0) if nsub > 1 else b_tiles[0]
    b_last = off
    o = _dot_nt(_pad_rows(q * jnp.exp(b)).astype(BF16), st.astype(BF16))[:CP]
    if nsub > 1:
        e_all = jnp.concatenate([jnp.broadcast_to(e, (SUBLANES, HEAD_DIM)) for e in ends], axis=0)
        ke = kk * jnp.exp(e_all - b)
        lhs, rhs = [], []
        for j in range(nsub - 1):
            lo, hi = j * SUBLANES, (j + 1) * SUBLANES
            lhs.append(jnp.where(row >= hi, q * jnp.exp(jnp.minimum(b - ends[j], 0.0)), 0.0))
            rhs.append(jnp.where((row >= lo) & (row < hi), ke, 0.0))
        a_off = _dot_nt(jnp.concatenate(lhs, axis=1).astype(BF16), jnp.concatenate(rhs, axis=1).astype(BF16))
        o = o + _dot(a_off.astype(BF16), v.astype(BF16))
        kd = ke * jnp.exp(b_last - e_all)
    else:
        kd = kk * jnp.exp(b_last - b)
    o_tiles = []
    for i in range(nsub):
        sl = slice(i * SUBLANES, (i + 1) * SUBLANES)
        oi, qi, bi = o[sl], q[sl], b_tiles[i]
        for s in range(i * SUBLANES, min((i + 1) * SUBLANES, C)):
            d = jnp.exp(jnp.minimum(bi - b[s:s + 1], 0.0))
            a = jnp.sum(qi * kk[s:s + 1] * d, axis=-1, keepdims=True)
            oi = oi + jnp.where(row8 >= s - i * SUBLANES, a, 0.0) * v[s:s + 1]
        o_tiles.append(oi)
    o = jnp.concatenate(o_tiles, axis=0) if nsub > 1 else o_tiles[0]
    st_new = st * jnp.exp(b_last) + _dot_tn(_pad_rows(v).astype(BF16), _pad_rows(kd).astype(BF16))
    return _rms(o, gn) * _sigmoid(ug), st_new


def _hgrn_kernel(uq_ref, uf_ref, uv_ref, ug_ref, lbl_ref, gn_ref, s0_ref, o_ref, sfin_ref,
                 st_ref, *, C, CP, nchunk, hb):
    t = pl.program_id(2)

    @pl.when(t == 0)
    def _():
        for h in range(hb):
            st_ref[h] = s0_ref[h].T

    lbl = lbl_ref[...]
    e = jnp.exp(lbl - jnp.max(lbl, axis=0, keepdims=True))
    lb = e[0:1] / jnp.sum(e, axis=0, keepdims=True)
    gn = gn_ref[...]

    def chunk(c, carry):
        r0 = pl.multiple_of(c * CP, CP)
        for h in range(hb):
            cs = slice(h * HEAD_DIM, (h + 1) * HEAD_DIM)
            out, st_new = _hgrn_chunk(uq_ref[pl.ds(r0, CP), cs], uf_ref[pl.ds(r0, CP), cs],
                                      uv_ref[pl.ds(r0, CP), cs], ug_ref[pl.ds(r0, CP), cs],
                                      lb[:, cs], gn[:, cs], st_ref[h], C, CP)
            st_ref[h] = st_new
            o_ref[pl.ds(r0, CP), cs] = out
        return carry

    lax.fori_loop(0, nchunk, chunk, 0)

    @pl.when(t == pl.num_programs(2) - 1)
    def _():
        for h in range(hb):
            sfin_ref[h] = st_ref[h].T


def _hgrn(u, lb_logits, gnorm, s0, B, T, H, t_valid):
    C = math.gcd(t_valid, HGRN_CHUNK)
    assert C == t_valid or T == t_valid
    CP = max(C, SUBLANES)
    tt = _pick(T, (512, 256, 128, 64, 32, 16, 8))
    nchunk = tt // CP
    if s0 is None:
        s0 = jnp.zeros((B, H, HEAD_DIM, HEAD_DIM), F32)
    W = H * HEAD_DIM
    hb = _pick(H, (HGRN_HEADS, 2, 1))
    hg = H // hb
    wb = hb * HEAD_DIM

    def col(part):
        return pl.BlockSpec((None, tt, wb), lambda b, h, t: (b, t, part * hg + h))

    return pl.pallas_call(
        functools.partial(_hgrn_kernel, C=C, CP=CP, nchunk=nchunk, hb=hb),
        grid=(B, hg, T // tt),
        in_specs=[col(0), col(1), col(2), col(3),
                  pl.BlockSpec((lb_logits.shape[0], wb), lambda b, h, t: (0, h)),
                  pl.BlockSpec((1, wb), lambda b, h, t: (0, h)),
                  pl.BlockSpec((None, hb, HEAD_DIM, HEAD_DIM), lambda b, h, t: (b, h, 0, 0))],
        out_specs=[pl.BlockSpec((None, tt, wb), lambda b, h, t: (b, t, h)),
                   pl.BlockSpec((None, hb, HEAD_DIM, HEAD_DIM), lambda b, h, t: (b, h, 0, 0))],
        out_shape=[jax.ShapeDtypeStruct((B, T, W), F32),
                   jax.ShapeDtypeStruct((B, H, HEAD_DIM, HEAD_DIM), F32)],
        scratch_shapes=[pltpu.VMEM((hb, HEAD_DIM, HEAD_DIM), F32)],
        compiler_params=_cparams(("parallel", "parallel", "arbitrary")),
    )(u, u, u, u, lb_logits, gnorm.reshape(1, W), s0)


def _memattn_kernel(q_ref, kv_ref, o_ref):
    for h in range(MEM_HEADS):
        sl = slice(h * HEAD_DIM, (h + 1) * HEAD_DIM)
        q = _pad_rows(q_ref[:, sl]).astype(BF16)
        if len(kv_ref.shape) == 4:
            k = kv_ref[:, 0, h, :].astype(BF16)
            v = kv_ref[:, 1, h, :].astype(BF16)
        else:
            k = kv_ref[:, sl].astype(BF16)
            v = kv_ref[:, MEM_WIDTH + h * HEAD_DIM:MEM_WIDTH + (h + 1) * HEAD_DIM].astype(BF16)
        s = _dot_nt(q, k) * SCALE
        p = jnp.exp(s - jnp.max(s, axis=-1, keepdims=True))
        l = jnp.sum(p, axis=-1, keepdims=True)
        o_ref[:, sl] = (_dot(p.astype(BF16), v) / l)[:q_ref.shape[0]]


def _memattn(u, qblock, mkv, B, T):
    tq = _pick(T, (512, 256, 128, 64, 32, 16, 8))
    zeros = (0,) * (mkv.ndim - 1)
    return pl.pallas_call(
        _memattn_kernel,
        grid=(B, T // tq),
        in_specs=[pl.BlockSpec((None, tq, MEM_WIDTH), lambda b, t: (b, t, qblock)),
                  pl.BlockSpec((None,) + mkv.shape[1:], lambda b, t: (b,) + zeros)],
        out_specs=pl.BlockSpec((None, tq, MEM_WIDTH), lambda b, t: (b, t, 0)),
        out_shape=jax.ShapeDtypeStruct((B, T, MEM_WIDTH), F32),
        compiler_params=_cparams(("parallel", "parallel")),
    )(u, mkv)


def _cmp_bias_kernel(pe_ref, wc_ref, o_ref):
    o_ref[...] = _dot(pe_ref[...].astype(BF16), wc_ref[...])


def _cmp_bias(pe2, wc):
    return pl.pallas_call(
        _cmp_bias_kernel,
        grid=(2,),
        in_specs=[pl.BlockSpec((None, 16, pe2.shape[2]), lambda r: (r, 0, 0)),
                  pl.BlockSpec((None,) + wc.shape[1:], lambda r: (r, 0, 0))],
        out_specs=pl.BlockSpec((None, 16, wc.shape[2]), lambda r: (r, 0, 0)),
        out_shape=jax.ShapeDtypeStruct((2, 16, wc.shape[2]), F32),
    )(pe2, wc)


def _cmp_finish(part, cb, w2):
    nch = part.shape[0]
    bias = cb[0:1, 0:HEAD_DIM] + cb[1:2, HEAD_DIM:2 * HEAD_DIM]
    pre = bias + part[:, 0:HEAD_DIM] + pltpu.roll(part[:, HEAD_DIM:2 * HEAD_DIM], nch - 1, 0)
    return _dot(_gelu(pre).astype(BF16), w2)


def _cmp_prompt_kernel(kv0_ref, kv1_ref, wc_ref, w2_ref, cb_ref, o_ref, x_ref, *, nch):
    for g, kv_ref in enumerate((kv0_ref, kv1_ref)):
        for s in range(CMP_STRIDE):
            x = kv_ref[pl.ds(s, nch, stride=CMP_STRIDE), :]
            x_ref[g * nch:(g + 1) * nch, s * HEAD_DIM:(s + 1) * HEAD_DIM] = x.astype(BF16)
    part = _dot(x_ref[...], wc_ref[...])
    for g in range(G_B):
        o_ref[g] = _cmp_finish(part[g * nch:(g + 1) * nch], cb_ref[...], w2_ref[...])


def _cmp_prompt(kv, wc, w2, cb, B, T):
    nch = T // CMP_STRIDE
    assert nch % 16 == 0
    return pl.pallas_call(
        functools.partial(_cmp_prompt_kernel, nch=nch),
        grid=(B, 2),
        in_specs=[pl.BlockSpec((T, HEAD_DIM), lambda b, r: (b, G_B * r)),
                  pl.BlockSpec((T, HEAD_DIM), lambda b, r: (b, G_B * r + 1)),
                  pl.BlockSpec((None,) + wc.shape[1:], lambda b, r: (r, 0, 0)),
                  pl.BlockSpec((None, HEAD_DIM, HEAD_DIM), lambda b, r: (r, 0, 0)),
                  pl.BlockSpec((None, 16, wc.shape[2]), lambda b, r: (r, 0, 0))],
        out_specs=pl.BlockSpec((None, None, G_B, nch, HEAD_DIM), lambda b, r: (b, r, 0, 0, 0)),
        out_shape=jax.ShapeDtypeStruct((B, 2, G_B, nch, HEAD_DIM), F32),
        scratch_shapes=[pltpu.VMEM((G_B * nch, CMP_STRIDE * HEAD_DIM), BF16)],
        compiler_params=_cparams(("parallel", "parallel")),
    )(kv, kv, wc, w2, cb)


def _stack_heads(q, hpg):
    return jnp.concatenate([q[:, r * HEAD_DIM:(r + 1) * HEAD_DIM] for r in range(hpg)],
                           axis=0).astype(BF16)


def _softmax_pieces(pieces, hpg, tq, slopes):
    outs, probs = [], [[] for _ in pieces]
    for r in range(hpg):
        ss = []
        for (s, dist, mask, _) in pieces:
            sr = s[r * tq:(r + 1) * tq] - slopes[r] * dist
            ss.append(jnp.where(mask, sr, NEG))
        m = ss[0].max(axis=-1, keepdims=True)
        for sr in ss[1:]:
            m = jnp.maximum(m, sr.max(axis=-1, keepdims=True))
        es = [jnp.where(mask, jnp.exp(sr - m), 0.0) for sr, (_, _, mask, _) in zip(ss, pieces)]
        l = es[0].sum(axis=-1, keepdims=True)
        for ee in es[1:]:
            l = l + ee.sum(axis=-1, keepdims=True)
        inv = 1.0 / jnp.maximum(l, 1e-30)
        for i, ee in enumerate(es):
            probs[i].append(ee * inv)
    probs = [jnp.concatenate(p, axis=0) for p in probs]
    o = _dot(probs[0].astype(BF16), pieces[0][3])
    for p, piece in zip(probs[1:], pieces[1:]):
        o = o + _dot(p.astype(BF16), piece[3])
    return probs, o


def _sel_score_mask(imp, msel, qpos, nsel, ns):
    hi = imp.astype(BF16).astype(F32)
    mid = (imp - hi).astype(BF16).astype(F32)
    lo = imp - hi - mid
    p = _dot_fewrows(hi, msel) + _dot_fewrows(mid, msel) + _dot_fewrows(lo, msel)
    t, nsp = p.shape
    if t == nsp:
        nsr = -(-ns // SUBLANES) * SUBLANES
        p = p.T[:nsr]
        j = lax.broadcasted_iota(jnp.int32, (nsr, 1), 0)
        cur = (qpos[0:1] + lax.broadcasted_iota(jnp.int32, (1, t), 1)) // SEL_BLOCK
    else:
        j = lax.broadcasted_iota(jnp.int32, (1, nsp), 1)
        cur = qpos // SEL_BLOCK
    valid = (j <= cur) & (j < ns)
    forced = (j == 0) | (j == cur) | (j == cur - 1)
    score = jnp.where(valid, p + jnp.where(forced, FORCE_BONUS, 0.0), NEG)
    cnt = jnp.zeros(score.shape, F32)
    for i in range(ns):
        other = score[i:i + 1] if t == nsp else score[:, i:i + 1]
        beats = (other > score) | ((other == score) & (i < j))
        cnt = cnt + jnp.where(beats, 1.0, 0.0)
    sel = jnp.where((cnt < nsel) & valid, 1.0, 0.0)
    if t == nsp:
        sel = jnp.concatenate([sel, jnp.zeros((nsp - nsr, t), F32)], axis=0).T if nsr < nsp else sel.T
    return sel


def _nsa_prompt_kernel(sl_ref, q_ref, gt_ref, kc_ref, vc_ref, ks_ref, vs_ref, kw_ref, vw_ref,
                       msel_ref, paug_ref, srow_ref, o_ref, *, tq, tk, hpg, nsel, ns, T):
    g = pl.program_id(1)
    qi = pl.program_id(2)
    q0 = qi * tq
    slopes = [sl_ref[g * hpg + r] for r in range(hpg)]
    qpos = q0 + lax.broadcasted_iota(jnp.int32, (tq, 1), 0)
    lane = lax.broadcasted_iota(jnp.int32, (1, HEAD_DIM), 1)
    Q = _stack_heads(q_ref[...] * SCALE, hpg)

    ncp = kc_ref.shape[0]
    end = lax.broadcasted_iota(jnp.int32, (1, ncp), 1) * CMP_STRIDE + (CMP_BLOCK - 1)
    dist = qpos - end
    sc = _dot_nt(Q, kc_ref[...].astype(BF16))
    (pc,), o_cmp = _softmax_pieces([(sc, dist.astype(F32), dist >= 0, vc_ref[...].astype(BF16))],
                                   hpg, tq, slopes)
    imp = pc[0:tq]
    for r in range(1, hpg):
        imp = imp + pc[r * tq:(r + 1) * tq]
    selm = _sel_score_mask(imp, msel_ref[...], qpos, nsel, ns)

    srows = [srow_ref[pl.ds(g * hpg + r, 1), :] for r in range(hpg)]
    selneg = jnp.where((selm > 0.5) | (lane >= ns), 0.0, MASK_BF16)
    q_sel = jnp.concatenate(
        [Q, jnp.concatenate([selneg + sr for sr in srows], axis=0).astype(BF16)], axis=1)
    q_win = jnp.concatenate(
        [Q, jnp.concatenate([jnp.broadcast_to(sr, (tq, HEAD_DIM)) for sr in srows], axis=0).astype(BF16)],
        axis=1)
    ones_col = jnp.where(lane == 0, 1.0, 0.0).astype(BF16)

    def attend_tile(q_aug, k_ref, v_ref, k0, n, bias, ms, acc):
        k_aug = jnp.concatenate([k_ref[pl.ds(k0, n), :].astype(BF16), paug_ref[pl.ds(k0, n), :]], axis=1)
        v_aug = jnp.concatenate([v_ref[pl.ds(k0, n), :].astype(BF16),
                                 jnp.broadcast_to(ones_col, (n, HEAD_DIM))], axis=1)
        s = _dot_nt(q_aug, k_aug)
        new_ms, ps, alphas = [], [], []
        for r in range(hpg):
            sr = s[r * tq:(r + 1) * tq]
            if bias is not None:
                sr = sr + bias
            m_new = jnp.maximum(ms[r], sr.max(axis=-1, keepdims=True))
            new_ms.append(m_new)
            ps.append(jnp.exp(sr - m_new))
            alphas.append(jnp.broadcast_to(jnp.exp(ms[r] - m_new), (tq, 2 * HEAD_DIM)))
        pv = _dot(jnp.concatenate(ps, axis=0).astype(BF16), v_aug)
        return tuple(new_ms), jnp.concatenate(alphas, axis=0) * acc + pv

    def finish(acc):
        return acc[:, :HEAD_DIM] / jnp.maximum(acc[:, HEAD_DIM:HEAD_DIM + 1], 1e-30)

    init = (tuple(jnp.full((tq, 1), NEG, F32) for _ in range(hpg)),
            jnp.zeros((hpg * tq, 2 * HEAD_DIM), F32))

    nkt = (q0 + tq + tk - 1) // tk

    def kt_body(kt, carry):
        return attend_tile(q_sel, ks_ref, vs_ref, pl.multiple_of(kt * tk, tk), tk, None, *carry)

    ms, acc = lax.fori_loop(0, nkt - 1, kt_body, init)
    k0 = pl.multiple_of((nkt - 1) * tk, tk)
    causal = jnp.where(qpos >= k0 + lax.broadcasted_iota(jnp.int32, (1, tk), 1), 0.0, NEG)
    _, acc = attend_tile(q_sel, ks_ref, vs_ref, k0, tk, causal, ms, acc)
    o_sel = finish(acc)

    wl = WINDOW + tq
    kstart = pl.multiple_of(jnp.maximum(q0 - WINDOW, 0), tq)
    dw = qpos - (kstart + lax.broadcasted_iota(jnp.int32, (1, wl), 1))
    band = jnp.where((dw >= 0) & (dw < WINDOW), 0.0, NEG)
    _, acc = attend_tile(q_win, kw_ref, vw_ref, kstart, wl, band, *init)
    o_win = finish(acc)

    gt = _sigmoid(gt_ref[...])
    for r in range(hpg):
        rows = slice(r * tq, (r + 1) * tq)
        o_ref[:, r * HEAD_DIM:(r + 1) * HEAD_DIM] = (
            gt[:, 3 * r:3 * r + 1] * o_cmp[rows] + gt[:, 3 * r + 1:3 * r + 2] * o_sel[rows]
            + gt[:, 3 * r + 2:3 * r + 3] * o_win[rows])


def _alibi_slopes_np(h_b):
    h = np.arange(1, h_b + 1, dtype=np.float32)
    return np.exp2(-8.0 * h / h_b).astype(np.float32)


def _alibi_slopes(h_b):
    return jnp.asarray(_alibi_slopes_np(h_b))


def _msel_matrix(ncp, nsp, nc, ns):
    i = np.arange(ncp)[:, None]
    j = np.arange(nsp)[None, :]
    r = SEL_BLOCK // CMP_STRIDE
    back = CMP_BLOCK // CMP_STRIDE - 1
    m = (i >= r * j - back) & (i <= r * j + r - 1) & (i < nc) & (j < ns)
    return jnp.asarray(m.astype(np.float32), dtype=BF16)


def _expand_matrix(nsteps, nsp, tk):
    s = np.arange(nsteps)[:, None, None]
    j = np.arange(nsp)[None, :, None]
    key = np.arange(tk)[None, None, :]
    return jnp.asarray(((s * tk + key) // SEL_BLOCK == j).astype(np.float32), dtype=BF16)


def _nsa_prompt(u, kv, kcv, B, T, hpg):
    tq, tk = 128, 512
    assert T % tk == 0 and T >= WINDOW + tq
    ncp = T // CMP_STRIDE
    nc = ncp - CMP_BLOCK // CMP_STRIDE + 1
    ns = -(-T // SEL_BLOCK)
    nsp = HEAD_DIM
    assert ns <= POS_LANE and T <= 256 * HEAD_DIM
    nsel = min(N_SEL, ns)
    nqt = T // tq
    qw = hpg * HEAD_DIM
    gate_block0 = (G_B * qw + MEM_WIDTH) // HEAD_DIM
    msel = _msel_matrix(ncp, nsp, nc, ns)
    kpos = np.arange(T)
    pa = np.zeros((T, HEAD_DIM), np.float32)
    pa[kpos, kpos // SEL_BLOCK] = 1.0
    pa[:, POS_LANE:POS_LANE + 3] = (kpos // HEAD_DIM * HEAD_DIM)[:, None]
    pa[:, POS_LANE + 3:POS_LANE + 6] = (kpos % HEAD_DIM)[:, None]
    paug = jnp.asarray(pa, dtype=BF16)
    sl = np.asarray(_alibi_slopes_np(G_B * hpg))
    s1 = sl.astype(ml_dtypes.bfloat16).astype(np.float32)
    s2 = (sl - s1).astype(ml_dtypes.bfloat16).astype(np.float32)
    s3 = (sl - s1 - s2).astype(ml_dtypes.bfloat16).astype(np.float32)
    sr = np.zeros((-(-G_B * hpg // SUBLANES) * SUBLANES, HEAD_DIM), np.float32)
    for c, piece in enumerate((s1, s2, s3, s1, s2, s3)):
        sr[:G_B * hpg, POS_LANE + c] = piece
    srow = jnp.asarray(sr)

    def kvcol(c):
        return pl.BlockSpec((T, HEAD_DIM), lambda b, g, t: (b, c + g))

    def cmpspec(r):
        return pl.BlockSpec((None, None, None, ncp, HEAD_DIM), lambda b, g, t: (b, r, g, 0, 0))

    return pl.pallas_call(
        functools.partial(_nsa_prompt_kernel, tq=tq, tk=tk, hpg=hpg, nsel=nsel, ns=ns, T=T),
        grid=(B, G_B, nqt),
        in_specs=[pl.BlockSpec(memory_space=pltpu.SMEM),
                  pl.BlockSpec((tq, qw), lambda b, g, t: (b * nqt + t, g)),
                  pl.BlockSpec((tq, HEAD_DIM), lambda b, g, t: (b * nqt + t, gate_block0 + g)),
                  cmpspec(0), cmpspec(1), kvcol(4), kvcol(6), kvcol(8), kvcol(10),
                  pl.BlockSpec(msel.shape, lambda b, g, t: (0, 0)),
                  pl.BlockSpec(paug.shape, lambda b, g, t: (0, 0)),
                  pl.BlockSpec(srow.shape, lambda b, g, t: (0, 0))],
        out_specs=pl.BlockSpec((tq, qw), lambda b, g, t: (b * nqt + t, g)),
        out_shape=jax.ShapeDtypeStruct((B * T, G_B * qw), F32),
        compiler_params=_cparams(("parallel", "parallel", "arbitrary")),
    )(_alibi_slopes(G_B * hpg), u, u, kcv, kcv, kv, kv, kv, kv, msel, paug, srow)


CMP_PAGES = 16
SEL_PAGES = 8


def _cmp_pages_kernel(pt_ref, *refs, npage):
    page_refs = refs[:npage]
    wc_ref, o_ref, x_ref, stage_ref = refs[npage:]
    cpp = PAGE_SIZE // CMP_STRIDE
    rows = npage * cpp
    for rt in range(2):
        for g in range(G_B):
            st = stage_ref.at[rt * G_B + g]
            for p in range(npage):
                st[p * PAGE_SIZE:(p + 1) * PAGE_SIZE, :] = page_refs[p][:, rt, g, :]
            for s in range(CMP_STRIDE):
                x_ref[rt, g * rows:(g + 1) * rows, s * HEAD_DIM:(s + 1) * HEAD_DIM] = (
                    st[pl.ds(s, rows, stride=CMP_STRIDE), :])
    for rt in range(2):
        part = _dot(x_ref[rt].astype(BF16), wc_ref[rt])
        for g in range(G_B):
            o_ref[rt, g] = part[g * rows:(g + 1) * rows]


def _cmp_pages(cache, page_table, wc):
    Bs, npg = page_table.shape
    npage = CMP_PAGES
    assert npg % npage == 0
    cpp = PAGE_SIZE // CMP_STRIDE
    rows = npage * cpp
    nch = npg * cpp

    def page_spec(p):
        return pl.BlockSpec((None, PAGE_SIZE, 2, G_B, HEAD_DIM),
                            lambda b, s, pt: (pt[b, s * npage + p], 0, 0, 0, 0))

    grid_spec = pltpu.PrefetchScalarGridSpec(
        num_scalar_prefetch=1,
        grid=(Bs, npg // npage),
        in_specs=[page_spec(p) for p in range(npage)]
        + [pl.BlockSpec(wc.shape, lambda b, s, pt: (0, 0, 0))],
        out_specs=pl.BlockSpec((None, 2, G_B, rows, wc.shape[2]), lambda b, s, pt: (b, 0, 0, s, 0)),
        scratch_shapes=[pltpu.VMEM((2, G_B * rows, CMP_STRIDE * HEAD_DIM), F32),
                        pltpu.VMEM((2 * G_B, npage * PAGE_SIZE, HEAD_DIM), F32)],
    )
    return pl.pallas_call(
        functools.partial(_cmp_pages_kernel, npage=npage),
        grid_spec=grid_spec,
        out_shape=jax.ShapeDtypeStruct((Bs, 2, G_B, nch, wc.shape[2]), F32),
        compiler_params=_cparams(("parallel", "arbitrary")),
    )(page_table, *([cache] * npage), wc)


def _nsa_sample_cmp_kernel(sl_ref, part_ref, cb_ref, w2_ref, q_ref, msel_ref, ocmp_ref, selm_ref,
                           *, tp, hpg, nsel, ns, past):
    g = pl.program_id(1)
    slopes = [sl_ref[g * hpg + r] for r in range(hpg)]
    kc = _cmp_finish(part_ref[0], cb_ref[0], w2_ref[0]).astype(BF16)
    vc = _cmp_finish(part_ref[1], cb_ref[1], w2_ref[1]).astype(BF16)
    nch = kc.shape[0]
    qpos = past + lax.broadcasted_iota(jnp.int32, (tp, 1), 0)
    Q = _stack_heads(q_ref[...], hpg)
    idx = lax.broadcasted_iota(jnp.int32, (1, nch), 1)
    dist = qpos - (idx * CMP_STRIDE + (CMP_BLOCK - 1))
    mask = (dist >= 0) & (idx < nch - 1)
    sc = _dot_nt(Q, kc) * SCALE
    (pc,), o_cmp = _softmax_pieces([(sc, dist.astype(F32), mask, vc)], hpg, tp, slopes)
    imp = pc[0:tp]
    for r in range(1, hpg):
        imp = imp + pc[r * tp:(r + 1) * tp]
    ocmp_ref[...] = o_cmp
    selm_ref[...] = _sel_score_mask(imp, msel_ref[...], qpos, nsel, ns)


def _nsa_sample_cmp(u, part, cb, w2, Bs, tp, hpg, past):
    nch = part.shape[3]
    nc = nch - CMP_BLOCK // CMP_STRIDE + 1
    ns = past // SEL_BLOCK + 1
    nsp = -(-ns // 128) * 128
    nsel = min(N_SEL, ns)
    qw = hpg * HEAD_DIM
    msel = _msel_matrix(nch, nsp, nc, ns)
    return pl.pallas_call(
        functools.partial(_nsa_sample_cmp_kernel, tp=tp, hpg=hpg, nsel=nsel, ns=ns, past=past),
        grid=(Bs, G_B),
        in_specs=[pl.BlockSpec(memory_space=pltpu.SMEM),
                  pl.BlockSpec((None, 2, None, nch, part.shape[4]), lambda b, g: (b, 0, g, 0, 0)),
                  pl.BlockSpec(cb.shape, lambda b, g: (0, 0, 0)),
                  pl.BlockSpec(w2.shape, lambda b, g: (0, 0, 0)),
                  pl.BlockSpec((None, tp, qw), lambda b, g: (b, 0, g)),
                  pl.BlockSpec(msel.shape, lambda b, g: (0, 0))],
        out_specs=[pl.BlockSpec((None, None, hpg * tp, HEAD_DIM), lambda b, g: (b, g, 0, 0)),
                   pl.BlockSpec((None, None, tp, nsp), lambda b, g: (b, g, 0, 0))],
        out_shape=[jax.ShapeDtypeStruct((Bs, G_B, hpg * tp, HEAD_DIM), F32),
                   jax.ShapeDtypeStruct((Bs, G_B, tp, nsp), F32)],
        compiler_params=_cparams(("parallel", "parallel")),
    )(_alibi_slopes(G_B * hpg), part, cb, w2, u, msel)


def _nsa_sample_kernel(pt_ref, sl_ref, *refs, npage, tp, hpg, past):
    page_refs = refs[:npage]
    q_ref, gt_ref, selm_ref, ocmp_ref, kvn_ref, cwin_ref, e_ref, o_ref, m_ref, l_ref, acc_ref = refs[npage:]
    s_id = pl.program_id(1)
    nsteps = pl.num_programs(1)
    tk = npage * PAGE_SIZE
    rows = hpg * tp
    qpos = past + lax.broadcasted_iota(jnp.int32, (tp, 1), 0)

    @pl.when(s_id == 0)
    def _():
        m_ref[...] = jnp.full(m_ref.shape, NEG, F32)
        l_ref[...] = jnp.zeros(l_ref.shape, F32)
        acc_ref[...] = jnp.zeros(acc_ref.shape, F32)

    kpos = s_id * tk + lax.broadcasted_iota(jnp.int32, (1, tk), 1)
    d = qpos - kpos
    df = d.astype(F32)
    for g in range(G_B):
        slopes = [sl_ref[g * hpg + r] for r in range(hpg)]
        Q = _stack_heads(q_ref[:, g * hpg * HEAD_DIM:(g + 1) * hpg * HEAD_DIM], hpg)
        kb = jnp.concatenate([page_refs[p][:, 0, g, :] for p in range(npage)], axis=0).astype(BF16)
        vb = jnp.concatenate([page_refs[p][:, 1, g, :] for p in range(npage)], axis=0).astype(BF16)
        s = _dot_nt(Q, kb) * SCALE
        mask = (_dot_fewrows(selm_ref[g], e_ref[...]) > 0.5) & (d >= 0)
        ps, alphas = [], []
        for r in range(hpg):
            hr = slice(r * tp, (r + 1) * tp)
            sr = jnp.where(mask, s[hr] - slopes[r] * df, NEG)
            m_old = m_ref[g, hr]
            m_new = jnp.maximum(m_old, sr.max(axis=-1, keepdims=True))
            alpha = jnp.exp(m_old - m_new)
            p = jnp.where(mask, jnp.exp(sr - m_new), 0.0)
            m_ref[g, hr] = m_new
            l_ref[g, hr] = alpha * l_ref[g, hr] + p.sum(axis=-1, keepdims=True)
            ps.append(p)
            alphas.append(jnp.broadcast_to(alpha, (tp, HEAD_DIM)))
        pv = _dot(jnp.concatenate(ps, axis=0).astype(BF16), vb)
        acc_ref[g] = jnp.concatenate(alphas, axis=0) * acc_ref[g] + pv

    @pl.when(s_id == nsteps - 1)
    def _():
        gt = _sigmoid(gt_ref[...])
        npad = HEAD_DIM
        kidx = lax.broadcasted_iota(jnp.int32, (1, npad), 1)
        dn = qpos - (past + kidx)
        new_ok = (kidx < tp) & (dn >= 0)
        zpad = jnp.zeros((npad - tp, HEAD_DIM), F32)
        nblk = past // SEL_BLOCK
        wpast = cwin_ref.shape[0]
        dwp = qpos - (past - wpast + lax.broadcasted_iota(jnp.int32, (1, wpast), 1))
        for g in range(G_B):
            slopes = [sl_ref[g * hpg + r] for r in range(hpg)]
            Q = _stack_heads(q_ref[:, g * hpg * HEAD_DIM:(g + 1) * hpg * HEAD_DIM], hpg)

            def newrows(c):
                blk = kvn_ref[:, (c * G_B + g) * HEAD_DIM:(c * G_B + g + 1) * HEAD_DIM]
                return jnp.concatenate([blk, zpad], axis=0).astype(BF16)

            kn, vn = newrows(2), newrows(3)
            sn = _dot_nt(Q, kn) * SCALE
            mask_n = new_ok & (selm_ref[g][:, nblk:nblk + 1] > 0.5)
            o_sel = []
            for r in range(hpg):
                hr = slice(r * tp, (r + 1) * tp)
                sr = jnp.where(mask_n, sn[hr] - slopes[r] * dn.astype(F32), NEG)
                m_old = m_ref[g, hr]
                m_new = jnp.maximum(m_old, sr.max(axis=-1, keepdims=True))
                alpha = jnp.exp(m_old - m_new)
                p = jnp.where(mask_n, jnp.exp(sr - m_new), 0.0)
                l = alpha * l_ref[g, hr] + p.sum(axis=-1, keepdims=True)
                acc = alpha * acc_ref[g, hr] + _dot_fewrows(p, vn)
                o_sel.append(acc / jnp.maximum(l, 1e-30))
            kwp = cwin_ref[:, 0, g, :].astype(BF16)
            vwp = cwin_ref[:, 1, g, :].astype(BF16)
            kwn, vwn = newrows(4), newrows(5)
            pieces = [(_dot_nt(Q, kwp) * SCALE, dwp.astype(F32), (dwp >= 0) & (dwp < WINDOW), vwp),
                      (_dot_nt(Q, kwn) * SCALE, dn.astype(F32), new_ok & (dn < WINDOW), vwn)]
            _, o_win = _softmax_pieces(pieces, hpg, tp, slopes)
            o_cmp = ocmp_ref[g]
            for r in range(hpg):
                hr = slice(r * tp, (r + 1) * tp)
                c = 3 * (g * hpg + r)
                col = (g * hpg + r) * HEAD_DIM
                o_ref[:, col:col + HEAD_DIM] = (gt[:, c:c + 1] * o_cmp[hr] + gt[:, c + 1:c + 2] * o_sel[r]
                                                + gt[:, c + 2:c + 3] * o_win[hr])


def _nsa_sample(u, cache, page_table, selm, o_cmp, kvn, cwin, Bs, tp, hpg, past):
    npg = page_table.shape[1]
    npage = SEL_PAGES
    assert npg % npage == 0
    nsteps = npg // npage
    tk = npage * PAGE_SIZE
    nsp = selm.shape[3]
    qw = G_B * hpg * HEAD_DIM
    e3 = _expand_matrix(nsteps, nsp, tk)
    gates = jnp.concatenate([u[:, :, qw + MEM_WIDTH + g * HEAD_DIM:qw + MEM_WIDTH + g * HEAD_DIM + 3 * hpg]
                             for g in range(G_B)]
                            + [jnp.zeros((Bs, tp, HEAD_DIM - 3 * hpg * G_B), F32)], axis=2)

    def page_spec(p):
        return pl.BlockSpec((None, PAGE_SIZE, 2, G_B, HEAD_DIM),
                            lambda b, s, pt: (pt[b, s * npage + p], 0, 1, 0, 0))

    grid_spec = pltpu.PrefetchScalarGridSpec(
        num_scalar_prefetch=1,
        grid=(Bs, nsteps),
        in_specs=[pl.BlockSpec(memory_space=pltpu.SMEM)]
        + [page_spec(p) for p in range(npage)]
        + [pl.BlockSpec((None, tp, qw), lambda b, s, pt: (b, 0, 0)),
           pl.BlockSpec((None, tp, HEAD_DIM), lambda b, s, pt: (b, 0, 0)),
           pl.BlockSpec((None, G_B, tp, nsp), lambda b, s, pt: (b, 0, 0, 0)),
           pl.BlockSpec((None, G_B, hpg * tp, HEAD_DIM), lambda b, s, pt: (b, 0, 0, 0)),
           pl.BlockSpec((None, tp, kvn.shape[2]), lambda b, s, pt: (b, 0, 0)),
           pl.BlockSpec((None,) + cwin.shape[1:], lambda b, s, pt: (b, 0, 0, 0, 0)),
           pl.BlockSpec((None, nsp, tk), lambda b, s, pt: (s, 0, 0))],
        out_specs=pl.BlockSpec((None, tp, qw), lambda b, s, pt: (b, 0, 0)),
        scratch_shapes=[pltpu.VMEM((G_B, hpg * tp, 1), F32), pltpu.VMEM((G_B, hpg * tp, 1), F32),
                        pltpu.VMEM((G_B, hpg * tp, HEAD_DIM), F32)],
    )
    return pl.pallas_call(
        functools.partial(_nsa_sample_kernel, npage=npage, tp=tp, hpg=hpg, past=past),
        grid_spec=grid_spec,
        out_shape=jax.ShapeDtypeStruct((Bs, tp, qw), F32),
        compiler_params=_cparams(("parallel", "arbitrary")),
    )(page_table, _alibi_slopes(G_B * hpg), *([cache] * npage), u, gates, selm, o_cmp, kvn, cwin, e3)


def _prep_weights(w_in_a, w_in_b, w_o, w_mem_kv, w_kv_b, w_cmp1, w_cmp2, w_ffn_in, w_ffn_out, cmp_pos, hpg):
    tokw = w_in_b.shape[2] - 3 * G_B * hpg - MEM_WIDTH
    D = w_in_b.shape[1]
    wb = w_in_b[0]
    gates = wb[:, tokw:tokw + 3 * G_B * hpg]
    gpad = jnp.zeros((D, HEAD_DIM - 3 * hpg), F32)
    wb2 = jnp.concatenate([wb[:, :tokw], wb[:, tokw + 3 * G_B * hpg:]]
                          + sum([[gates[:, g * 3 * hpg:(g + 1) * 3 * hpg], gpad] for g in range(G_B)], []),
                          axis=1)
    r = CMP_BLOCK // CMP_STRIDE
    flat = CMP_STRIDE * HEAD_DIM
    wc = w_cmp1.reshape(2, r, flat, HEAD_DIM).transpose(0, 2, 1, 3).reshape(2, flat, r * HEAD_DIM)
    pe2 = jnp.concatenate([cmp_pos.reshape(2, r, flat), jnp.zeros((2, 16 - r, flat), F32)], axis=1)
    return dict(w_in_a=w_in_a.astype(BF16), w_in_b=wb2.astype(BF16), w_o=w_o.astype(BF16),
                w_mem_kv=w_mem_kv.astype(BF16), w_kv_b=w_kv_b.astype(BF16), wc=wc.astype(BF16),
                w2=w_cmp2.astype(BF16), w_ffn_in=w_ffn_in.astype(BF16), w_ffn_out=w_ffn_out.astype(BF16),
                pe2=pe2)


def _layer_tail(h, mix, mo, l, prm, conv_buf, B, T):
    g = prm['norm_gains'][l]
    h = _mmres(mix, mo, prm['w_o'][l], h, g[1])
    y, new_buf = _ffn_in(h, g[2], prm['w_ffn_in'][l], prm['w_ffn_conv'][l], prm['b_ffn_conv'][l],
                         None if conv_buf is None else conv_buf[l], B, T)
    h = _mmres(y, None, prm['w_ffn_out'][l], h, g[3])
    return h, new_buf


def _trunk(x, mem_kv, hgrn_s0, conv_buf, prm, B, T, t_valid, nsa_fn):
    D = x.shape[-1]
    H = prm['hgrn_norm'].shape[1]
    h = x.reshape(B * T, D)
    u = _mm(h, prm['norm_gains'][0, 0], prm['w_in_a'][0]).reshape(B, T, -1)
    mix, s_fin = _hgrn(u, prm['lb_logits'], prm['hgrn_norm'][0], None if hgrn_s0 is None else hgrn_s0[0],
                       B, T, H, t_valid)
    mo = _memattn(u, 4 * H * HEAD_DIM // MEM_WIDTH, mem_kv[0], B, T)
    h, buf0 = _layer_tail(h, mix.reshape(B * T, -1), mo.reshape(B * T, -1), 0, prm, conv_buf, B, T)
    kv = _mm(h, prm['kv_norm'], prm['w_kv_b'])
    u = _mm(h, prm['norm_gains'][1, 0], prm['w_in_b'])
    mix = nsa_fn(u, kv)
    mo = _memattn(u.reshape(B, T, -1), H * HEAD_DIM // MEM_WIDTH, mem_kv[1], B, T)
    h, buf1 = _layer_tail(h, mix.reshape(B * T, -1), mo.reshape(B * T, -1), 1, prm, conv_buf, B, T)
    return h.reshape(B, T, D), s_fin, (buf0, buf1), kv


def kernel(x_prompt, x_sample, mem_prompt, state_hgrn, cache_conv, cache_mem, cache_kv, cache_win,
           page_table, norm_gains, w_in_a, lb_logits, hgrn_norm, w_in_b, w_o, w_mem_kv, kv_norm,
           w_kv_b, cmp_pos, w_cmp1, w_cmp2, w_ffn_in, w_ffn_conv, b_ffn_conv, w_ffn_out):
    Bp, Tp, D = x_prompt.shape
    Bs, Ts, _ = x_sample.shape
    depth = norm_gains.shape[0]
    assert depth == 2 and w_in_a.shape[0] == 1 and w_in_b.shape[0] == 1
    H = hgrn_norm.shape[1]
    hpg = H // G_B
    ml = mem_prompt.shape[1]
    n_rows = cache_kv.shape[2]

    prm = _prep_weights(w_in_a, w_in_b, w_o, w_mem_kv, w_kv_b, w_cmp1, w_cmp2, w_ffn_in, w_ffn_out,
                        cmp_pos, hpg)
    prm.update(norm_gains=norm_gains, lb_logits=lb_logits, hgrn_norm=hgrn_norm, kv_norm=kv_norm,
               w_ffn_conv=w_ffn_conv, b_ffn_conv=b_ffn_conv)
    cb = _cmp_bias(prm['pe2'], prm['wc'])

    memx = mem_prompt.reshape(Bp * ml, D)
    mem_kv_p = jnp.stack([_mm(memx, None, prm['w_mem_kv'][l], norm=False) for l in range(depth)])
    mem_kv_p = mem_kv_p.reshape(depth, Bp, ml, 2 * MEM_WIDTH)

    def nsa_p(u, kv):
        kcv = _cmp_prompt(kv, prm['wc'], prm['w2'], cb, Bp, Tp)
        return _nsa_prompt(u, kv, kcv, Bp, Tp, hpg)

    y_p, hgrn_p, conv_p, kv_p = _trunk(x_prompt, mem_kv_p, None, None, prm, Bp, Tp, Tp, nsa_p)

    tp = -(-Ts // SUBLANES) * SUBLANES
    past = page_table.shape[1] * PAGE_SIZE
    assert Ts < CMP_STRIDE and tp <= SEL_BLOCK and past >= WINDOW
    xs = jnp.concatenate([x_sample, jnp.zeros((Bs, tp - Ts, D), F32)], axis=1)

    def nsa_s(u, kv):
        u3 = u.reshape(Bs, tp, -1)
        part = _cmp_pages(cache_kv, page_table, prm['wc'])
        o_cmp, selm = _nsa_sample_cmp(u3, part, cb, prm['w2'], Bs, tp, hpg, past)
        return _nsa_sample(u3, cache_kv, page_table, selm, o_cmp, kv.reshape(Bs, tp, -1), cache_win,
                           Bs, tp, hpg, past)

    y_s, hgrn_s, conv_s, kv_s = _trunk(xs, cache_mem, state_hgrn, cache_conv, prm, Bs, tp, Ts, nsa_s)

    nkv = n_rows * G_B * HEAD_DIM
    kv_p3 = kv_p.reshape(Bp, Tp, -1)
    kv_s3 = kv_s.reshape(Bs, tp, -1)[:, :Ts]
    wl = min(WINDOW, Tp)
    return (
        y_p,
        y_s[:, :Ts],
        hgrn_p[None],
        hgrn_s[None],
        jnp.stack(conv_p),
        jnp.stack([c[:, Ts - (CONV_W - 1):Ts] for c in conv_s]),
        mem_kv_p.reshape(depth, Bp, ml, 2, MEM_HEADS, HEAD_DIM),
        kv_p3[:, :, :nkv].reshape(Bp, Tp // PAGE_SIZE, PAGE_SIZE, n_rows, G_B, HEAD_DIM),
        kv_s3[:, :, :nkv].reshape(Bs, Ts, n_rows, G_B, HEAD_DIM),
        kv_p3[:, Tp - wl:, nkv:].reshape(Bp, wl, 2, G_B, HEAD_DIM),
        kv_s3[:, :, nkv:].reshape(Bs, Ts, 2, G_B, HEAD_DIM),
    )
```

```python
import functools
import math

import ml_dtypes
import numpy as np
import jax
import jax.numpy as jnp
from jax import lax
from jax.experimental import pallas as pl
from jax.experimental.pallas import tpu as pltpu

F32 = jnp.float32
BF16 = jnp.bfloat16

HEAD_DIM = 128
MEM_HEADS = 4
MEM_WIDTH = MEM_HEADS * HEAD_DIM
G_B = 2
HGRN_CHUNK = 32
CMP_BLOCK = 32
CMP_STRIDE = 16
SEL_BLOCK = 64
N_SEL = 16
WINDOW = 512
PAGE_SIZE = 128
FORCE_BONUS = 1e4
CONV_W = 3
EPS = 1e-6
NEG = -1e30
MASK_BF16 = -2.0 ** 100
POS_LANE = 120
SCALE = HEAD_DIM ** -0.5
SUBLANES = 8
HGRN_HEADS = 6
VMEM_LIMIT = 56 * 1024 * 1024


def _cparams(sem):
    return pltpu.CompilerParams(dimension_semantics=sem, vmem_limit_bytes=VMEM_LIMIT)


def _dot(a, b):
    return jnp.dot(a, b, preferred_element_type=F32)


def _dot_nt(a, b):
    return lax.dot_general(a, b, (((1,), (1,)), ((), ())), preferred_element_type=F32)


def _dot_tn(a, b):
    return lax.dot_general(a, b, (((0,), (0,)), ((), ())), preferred_element_type=F32)


BF16_ROWS = 16


def _pad_rows(x):
    n = x.shape[0]
    if n % BF16_ROWS == 0:
        return x
    return jnp.concatenate([x, jnp.zeros((BF16_ROWS - n % BF16_ROWS,) + x.shape[1:], x.dtype)], axis=0)


def _dot_fewrows(a, b):
    return _dot(_pad_rows(a).astype(BF16), b)[:a.shape[0]]


def _rms(x, g):
    return x * lax.rsqrt(jnp.mean(x * x, axis=-1, keepdims=True) + EPS) * g


def _sigmoid(x):
    return 0.5 * jnp.tanh(0.5 * x) + 0.5


def _gelu(x):
    return 0.5 * x * (1.0 + jnp.tanh(0.7978845608028654 * (x + 0.044715 * (x * x * x))))


def _pick(n, cands):
    for c in cands:
        if n % c == 0:
            return c
    raise ValueError(f"no tile for {n} in {cands}")


def _mm_kernel(x_ref, g_ref, w_ref, o_ref, *rest, norm, rows, slabs):
    xn_ref = rest[-1]
    tm = x_ref.shape[0]
    j = pl.program_id(1)

    @pl.when(j == 0)
    def _():
        def body(c, carry):
            r = pl.multiple_of(c * rows, rows)
            x = x_ref[pl.ds(r, rows), :]
            if norm:
                x = _rms(x, g_ref[...])
            xn_ref[pl.ds(r, rows), :] = x.astype(BF16)
            return carry
        lax.fori_loop(0, tm // rows, body, 0)

    res = _dot(xn_ref[...], w_ref[...])
    o_ref[...] = res
    if slabs:
        per_step = res.shape[1] // HEAD_DIM
        first = 0
        for s_ref, nslab in zip(rest[:-1], slabs):
            steps = nslab // per_step

            @pl.when((j >= first) & (j < first + steps))
            def _(s_ref=s_ref, nslab=nslab, first=first):
                for c in range(per_step):
                    s_ref[pl.ds((j - first) * per_step + c, tm, stride=nslab), :] = (
                        res[:, c * HEAD_DIM:(c + 1) * HEAD_DIM])
            first += steps


def _mm(x, g, w, *, norm=True, slabs=()):
    M, K = x.shape
    N = w.shape[1]
    tm = _pick(M, (1024, 512, 256, 128, 64, 32, 16))
    tn = _pick(N, (512, 768, 256, 128))
    rows = min(tm, 128)
    if g is None:
        g = jnp.ones((K,), F32)
    out_specs = [pl.BlockSpec((tm, tn), lambda i, j: (i, j))]
    out_shape = [jax.ShapeDtypeStruct((M, N), F32)]
    for n in slabs:
        assert (n * HEAD_DIM) % tn == 0
        out_specs.append(pl.BlockSpec((tm * n, HEAD_DIM), lambda i, j: (i, 0)))
        out_shape.append(jax.ShapeDtypeStruct((M * n, HEAD_DIM), F32))
    assert sum(slabs) in (0, N // HEAD_DIM)
    out = pl.pallas_call(
        functools.partial(_mm_kernel, norm=norm, rows=rows, slabs=tuple(slabs)),
        grid=(M // tm, N // tn),
        in_specs=[pl.BlockSpec((tm, K), lambda i, j: (i, 0)),
                  pl.BlockSpec((1, K), lambda i, j: (0, 0)),
                  pl.BlockSpec((K, tn), lambda i, j: (0, j))],
        out_specs=out_specs,
        out_shape=out_shape,
        scratch_shapes=[pltpu.VMEM((tm, K), BF16)],
        compiler_params=_cparams(("parallel", "arbitrary")),
    )(x, g.reshape(1, K), w)
    return out if slabs else out[0]


def _mmres_kernel(*refs, nk, two):
    if two:
        a1_ref, a2_ref, w_ref, h_ref, g_ref, o_ref = refs
        a = jnp.concatenate([a1_ref[...].astype(BF16), a2_ref[...].astype(BF16)], axis=1)
    else:
        a1_ref, w_ref, h_ref, g_ref, o_ref = refs
        a = a1_ref[...].astype(BF16)
    k = pl.program_id(1)
    part = _dot(a, w_ref[...])
    if nk == 1:
        o_ref[...] = h_ref[...] + _rms(part, g_ref[...])
    else:
        @pl.when(k == 0)
        def _():
            o_ref[...] = part

        @pl.when((k > 0) & (k < nk - 1))
        def _():
            o_ref[...] += part

        @pl.when(k == nk - 1)
        def _():
            o_ref[...] = h_ref[...] + _rms(o_ref[...] + part, g_ref[...])


def _mmres(a1, a2, w, h, g):
    M, N = h.shape
    K = w.shape[0]
    two = a2 is not None
    nk = 1 if two else 2
    tk = K // nk
    assert K % nk == 0 and tk % 128 == 0
    tm = _pick(M, (512, 256, 128, 64, 32, 16))
    if two:
        assert a1.shape[1] + a2.shape[1] == K
        in_specs = [pl.BlockSpec((tm, a1.shape[1]), lambda i, k: (i, 0)),
                    pl.BlockSpec((tm, a2.shape[1]), lambda i, k: (i, 0))]
        args = [a1, a2]
    else:
        in_specs = [pl.BlockSpec((tm, tk), lambda i, k: (i, k))]
        args = [a1]
    in_specs += [pl.BlockSpec((tk, N), lambda i, k: (k, 0)),
                 pl.BlockSpec((tm, N), lambda i, k: (i, 0)),
                 pl.BlockSpec((1, N), lambda i, k: (0, 0))]
    args += [w, h, g.reshape(1, N)]
    return pl.pallas_call(
        functools.partial(_mmres_kernel, nk=nk, two=two),
        grid=(M // tm, nk),
        in_specs=in_specs,
        out_specs=pl.BlockSpec((tm, N), lambda i, k: (i, 0)),
        out_shape=jax.ShapeDtypeStruct((M, N), F32),
        compiler_params=_cparams(("parallel", "arbitrary")),
    )(*args)


def _ffn_in_kernel(x_ref, g_ref, wa_ref, wb_ref, wc_ref, bc_ref, p1_ref, p2_ref,
                   y_ref, a_ref, xn_ref, carry_ref, *, T, rows, rparts, cparts):
    tm = x_ref.shape[0]
    i = pl.program_id(0)
    j = pl.program_id(1)

    @pl.when(j == 0)
    def _():
        def body(c, carry):
            r = pl.multiple_of(c * rows, rows)
            xn_ref[pl.ds(r, rows), :] = _rms(x_ref[pl.ds(r, rows), :], g_ref[...]).astype(BF16)
            return carry
        lax.fori_loop(0, tm // rows, body, 0)

    wc = wc_ref[...]
    bc = bc_ref[...]
    if T >= tm:
        tiles_per_batch = T // tm

        @pl.when(i % tiles_per_batch == 0)
        def _():
            carry_ref[j] = p1_ref[...]

        prev = carry_ref[j]
        pr, pc = tm // rparts, wa_ref.shape[1] // cparts
        row = lax.broadcasted_iota(jnp.int32, (pr, 1), 0)
        for ch in range(cparts):
            cs = slice(ch * pc, (ch + 1) * pc)
            older, newer = prev[SUBLANES - 2:SUBLANES - 1, cs], prev[SUBLANES - 1:SUBLANES, cs]
            for rh in range(rparts):
                rs = slice(rh * pr, (rh + 1) * pr)
                xn = xn_ref[rs, :]
                a = _dot(xn, wa_ref[:, cs])
                b = _dot(xn, wb_ref[:, cs])
                a1 = jnp.where(row == 0, newer, pltpu.roll(a, 1, 0))
                a2 = jnp.where(row == 0, older, jnp.where(row == 1, newer, pltpu.roll(a, 2, 0)))
                c = bc[:, cs] + a2 * wc[0:1, cs] + a1 * wc[1:2, cs] + a * wc[2:3, cs]
                y_ref[rs, cs] = (_gelu(c) * b).astype(BF16)
                older, newer = a[pr - 2:pr - 1], a[pr - 1:pr]
            carry_ref[j, :, cs] = a[pr - SUBLANES:pr]
            a_ref[:, cs] = a[pr - SUBLANES:pr]
    else:
        xn = xn_ref[...]
        a = _dot(xn, wa_ref[...])
        b = _dot(xn, wb_ref[...])
        tmod = lax.broadcasted_iota(jnp.int32, (tm, 1), 0) % T
        a1 = jnp.where(tmod == 0, p1_ref[...], pltpu.roll(a, 1, 0))
        a2 = jnp.where(tmod < 2, p2_ref[...], pltpu.roll(a, 2, 0))
        a_ref[...] = a
        c = bc + a2 * wc[0:1] + a1 * wc[1:2] + a * wc[2:3]
        y_ref[...] = (_gelu(c) * b).astype(BF16)


def _ffn_in(x, g, w, wconv, bconv, buf, B, T):
    M, K = x.shape
    F = w.shape[1] // 2
    tn = 512
    nj = F // tn
    assert F % tn == 0
    if buf is None:
        buf = jnp.zeros((B, CONV_W - 1, F), F32)
    wc = jnp.concatenate([wconv, jnp.zeros((SUBLANES - CONV_W, F), F32)], axis=0)
    if T >= 512:
        tm = _pick(T, (1024, 512))
        tpb = T // tm
        p1 = jnp.concatenate([jnp.zeros((B, SUBLANES - 2, F), F32), buf], axis=1)
        p2 = p1
        p_specs = [pl.BlockSpec((None, SUBLANES, tn), lambda i, j: (i // tpb, 0, j))] * 2
        a_shape = jax.ShapeDtypeStruct((M // tm, SUBLANES, F), F32)
        a_spec = pl.BlockSpec((None, SUBLANES, tn), lambda i, j: (i, 0, j))
    else:
        tm = M
        assert T >= 2 and M % T == 0
        z = jnp.zeros((B, 1, F), F32)
        p1 = jnp.concatenate([buf[:, 1:2]] + [z] * (T - 1), axis=1).reshape(M, F)
        p2 = jnp.concatenate([buf[:, 0:1], buf[:, 1:2]] + [z] * (T - 2), axis=1).reshape(M, F)
        p_specs = [pl.BlockSpec((tm, tn), lambda i, j: (i, j))] * 2
        a_shape = jax.ShapeDtypeStruct((M, F), F32)
        a_spec = pl.BlockSpec((tm, tn), lambda i, j: (i, j))
    y, a = pl.pallas_call(
        functools.partial(_ffn_in_kernel, T=T, rows=min(tm, 128), rparts=max(tm // 512, 1), cparts=2),
        grid=(M // tm, nj),
        in_specs=[pl.BlockSpec((tm, K), lambda i, j: (i, 0)),
                  pl.BlockSpec((1, K), lambda i, j: (0, 0)),
                  pl.BlockSpec((K, tn), lambda i, j: (0, j)),
                  pl.BlockSpec((K, tn), lambda i, j: (0, j + nj)),
                  pl.BlockSpec((SUBLANES, tn), lambda i, j: (0, j)),
                  pl.BlockSpec((1, tn), lambda i, j: (0, j))] + p_specs,
        out_specs=[pl.BlockSpec((tm, tn), lambda i, j: (i, j)), a_spec],
        out_shape=[jax.ShapeDtypeStruct((M, F), BF16), a_shape],
        scratch_shapes=[pltpu.VMEM((tm, K), BF16), pltpu.VMEM((nj, SUBLANES, tn), F32)],
        compiler_params=_cparams(("arbitrary", "arbitrary")),
    )(x, g.reshape(1, K), w, w, wc, bconv.reshape(1, F), p1, p2)
    if T >= 512:
        new_buf = a.reshape(B, T // tm, SUBLANES, F)[:, -1, SUBLANES - (CONV_W - 1):]
    else:
        new_buf = a.reshape(B, T, F)
    return y, new_buf


def _cumsum_tile(x):
    row = lax.broadcasted_iota(jnp.int32, (SUBLANES, 1), 0)
    d = 1
    while d < SUBLANES:
        x = x + jnp.where(row >= d, pltpu.roll(x, d, 0), 0.0)
        d *= 2
    return x


def _hgrn_chunk(uq, uf, v, ug, lb, gn, st, C, CP):
    nsub = CP // SUBLANES
    row = lax.broadcasted_iota(jnp.int32, (CP, 1), 0)
    row8 = row[0:SUBLANES]
    q = uq * _sigmoid(uq)
    fg = lb + (1.0 - lb) * _sigmoid(uf)
    kk = 1.0 - fg
    lf = jnp.log(fg)
    if C < CP:
        kk = jnp.where(row < C, kk, 0.0)
        lf = jnp.where(row < C, lf, 0.0)
    b_tiles, ends, off = [], [], None
    for i in range(nsub):
        bi = _cumsum_tile(lf[i * SUBLANES:(i + 1) * SUBLANES])
        if off is not None:
            bi = bi + off
        off = bi[SUBLANES - 1:SUBLANES]
        b_tiles.append(bi)
        ends.append(off)
    b = jnp.concatenate(b_tiles, axis=0) if nsub > 1 else b_tiles[0]
    b_last = off
    o = _dot_nt(_pad_rows(q * jnp.exp(b)).astype(BF16), st.astype(BF16))[:CP]
    if nsub > 1:
        e_all = jnp.concatenate([jnp.broadcast_to(e, (SUBLANES, HEAD_DIM)) for e in ends], axis=0)
        ke = kk * jnp.exp(e_all - b)
        lhs, rhs = [], []
        for j in range(nsub - 1):
            lo, hi = j * SUBLANES, (j + 1) * SUBLANES
            lhs.append(jnp.where(row >= hi, q * jnp.exp(jnp.minimum(b - ends[j], 0.0)), 0.0))
            rhs.append(jnp.where((row >= lo) & (row < hi), ke, 0.0))
        a_off = _dot_nt(jnp.concatenate(lhs, axis=1).astype(BF16), jnp.concatenate(rhs, axis=1).astype(BF16))
        o = o + _dot(a_off.astype(BF16), v.astype(BF16))
        kd = ke * jnp.exp(b_last - e_all)
    else:
        kd = kk * jnp.exp(b_last - b)
    o_tiles = []
    for i in range(nsub):
        sl = slice(i * SUBLANES, (i + 1) * SUBLANES)
        oi, qi, bi = o[sl], q[sl], b_tiles[i]
        for s in range(i * SUBLANES, min((i + 1) * SUBLANES, C)):
            d = jnp.exp(jnp.minimum(bi - b[s:s + 1], 0.0))
            a = jnp.sum(qi * kk[s:s + 1] * d, axis=-1, keepdims=True)
            oi = oi + jnp.where(row8 >= s - i * SUBLANES, a, 0.0) * v[s:s + 1]
        o_tiles.append(oi)
    o = jnp.concatenate(o_tiles, axis=0) if nsub > 1 else o_tiles[0]
    st_new = st * jnp.exp(b_last) + _dot_tn(_pad_rows(v).astype(BF16), _pad_rows(kd).astype(BF16))
    return _rms(o, gn) * _sigmoid(ug), st_new


def _hgrn_kernel(uq_ref, uf_ref, uv_ref, ug_ref, lbl_ref, gn_ref, s0_ref, o_ref, sfin_ref,
                 st_ref, *, C, CP, nchunk, hb):
    t = pl.program_id(2)

    @pl.when(t == 0)
    def _():
        for h in range(hb):
            st_ref[h] = s0_ref[h].T

    lbl = lbl_ref[...]
    e = jnp.exp(lbl - jnp.max(lbl, axis=0, keepdims=True))
    lb = e[0:1] / jnp.sum(e, axis=0, keepdims=True)
    gn = gn_ref[...]

    def chunk(c, carry):
        r0 = pl.multiple_of(c * CP, CP)
        for h in range(hb):
            cs = slice(h * HEAD_DIM, (h + 1) * HEAD_DIM)
            out, st_new = _hgrn_chunk(uq_ref[pl.ds(r0, CP), cs], uf_ref[pl.ds(r0, CP), cs],
                                      uv_ref[pl.ds(r0, CP), cs], ug_ref[pl.ds(r0, CP), cs],
                                      lb[:, cs], gn[:, cs], st_ref[h], C, CP)
            st_ref[h] = st_new
            o_ref[pl.ds(r0, CP), cs] = out
        return carry

    lax.fori_loop(0, nchunk, chunk, 0)

    @pl.when(t == pl.num_programs(2) - 1)
    def _():
        for h in range(hb):
            sfin_ref[h] = st_ref[h].T


def _hgrn(u, lb_logits, gnorm, s0, B, T, H, t_valid):
    C = math.gcd(t_valid, HGRN_CHUNK)
    assert C == t_valid or T == t_valid
    CP = max(C, SUBLANES)
    tt = _pick(T, (512, 256, 128, 64, 32, 16, 8))
    nchunk = tt // CP
    if s0 is None:
        s0 = jnp.zeros((B, H, HEAD_DIM, HEAD_DIM), F32)
    W = H * HEAD_DIM
    hb = _pick(H, (HGRN_HEADS, 2, 1))
    hg = H // hb
    wb = hb * HEAD_DIM

    def col(part):
        return pl.BlockSpec((None, tt, wb), lambda b, h, t: (b, t, part * hg + h))

    return pl.pallas_call(
        functools.partial(_hgrn_kernel, C=C, CP=CP, nchunk=nchunk, hb=hb),
        grid=(B, hg, T // tt),
        in_specs=[col(0), col(1), col(2), col(3),
                  pl.BlockSpec((lb_logits.shape[0], wb), lambda b, h, t: (0, h)),
                  pl.BlockSpec((1, wb), lambda b, h, t: (0, h)),
                  pl.BlockSpec((None, hb, HEAD_DIM, HEAD_DIM), lambda b, h, t: (b, h, 0, 0))],
        out_specs=[pl.BlockSpec((None, tt, wb), lambda b, h, t: (b, t, h)),
                   pl.BlockSpec((None, hb, HEAD_DIM, HEAD_DIM), lambda b, h, t: (b, h, 0, 0))],
        out_shape=[jax.ShapeDtypeStruct((B, T, W), F32),
                   jax.ShapeDtypeStruct((B, H, HEAD_DIM, HEAD_DIM), F32)],
        scratch_shapes=[pltpu.VMEM((hb, HEAD_DIM, HEAD_DIM), F32)],
        compiler_params=_cparams(("parallel", "parallel", "arbitrary")),
    )(u, u, u, u, lb_logits, gnorm.reshape(1, W), s0)


def _memattn_kernel(q_ref, kv_ref, o_ref):
    for h in range(MEM_HEADS):
        sl = slice(h * HEAD_DIM, (h + 1) * HEAD_DIM)
        q = _pad_rows(q_ref[:, sl]).astype(BF16)
        if kv_ref.shape[1] == HEAD_DIM:
            ml = kv_ref.shape[0] // (2 * MEM_HEADS)
            k = kv_ref[pl.ds(h, ml, stride=2 * MEM_HEADS), :].astype(BF16)
            v = kv_ref[pl.ds(MEM_HEADS + h, ml, stride=2 * MEM_HEADS), :].astype(BF16)
        else:
            k = kv_ref[:, sl].astype(BF16)
            v = kv_ref[:, MEM_WIDTH + h * HEAD_DIM:MEM_WIDTH + (h + 1) * HEAD_DIM].astype(BF16)
        s = _dot_nt(q, k) * SCALE
        p = jnp.exp(s - jnp.max(s, axis=-1, keepdims=True))
        l = jnp.sum(p, axis=-1, keepdims=True)
        o_ref[:, sl] = (_dot(p.astype(BF16), v) / l)[:q_ref.shape[0]]


def _memattn(u, qblock, mkv, layer, B, T):
    tq = _pick(T, (512, 256, 128, 64, 32, 16, 8))
    if isinstance(mkv, (list, tuple)):
        mkv = mkv[layer]
        kv_spec = pl.BlockSpec((None,) + mkv.shape[1:], lambda b, t: (b, 0, 0))
    else:
        kv_spec = pl.BlockSpec((None, None) + mkv.shape[2:], lambda b, t: (layer, b, 0, 0))
    return pl.pallas_call(
        _memattn_kernel,
        grid=(B, T // tq),
        in_specs=[pl.BlockSpec((None, tq, MEM_WIDTH), lambda b, t: (b, t, qblock)), kv_spec],
        out_specs=pl.BlockSpec((None, tq, MEM_WIDTH), lambda b, t: (b, t, 0)),
        out_shape=jax.ShapeDtypeStruct((B, T, MEM_WIDTH), F32),
        compiler_params=_cparams(("parallel", "parallel")),
    )(u, mkv)


def _cmp_bias_kernel(pe_ref, wc_ref, o_ref):
    o_ref[...] = _dot(pe_ref[...].astype(BF16), wc_ref[...])


def _cmp_bias(pe2, wc):
    return pl.pallas_call(
        _cmp_bias_kernel,
        grid=(2,),
        in_specs=[pl.BlockSpec((None, 16, pe2.shape[2]), lambda r: (r, 0, 0)),
                  pl.BlockSpec((None,) + wc.shape[1:], lambda r: (r, 0, 0))],
        out_specs=pl.BlockSpec((None, 16, wc.shape[2]), lambda r: (r, 0, 0)),
        out_shape=jax.ShapeDtypeStruct((2, 16, wc.shape[2]), F32),
    )(pe2, wc)


def _cmp_finish(part, cb, w2):
    nch = part.shape[0]
    bias = cb[0:1, 0:HEAD_DIM] + cb[1:2, HEAD_DIM:2 * HEAD_DIM]
    pre = bias + part[:, 0:HEAD_DIM] + pltpu.roll(part[:, HEAD_DIM:2 * HEAD_DIM], nch - 1, 0)
    return _dot(_gelu(pre).astype(BF16), w2)


def _cmp_prompt_kernel(kv0_ref, kv1_ref, wc_ref, w2_ref, cb_ref, o_ref, x_ref, *, nch):
    for g, kv_ref in enumerate((kv0_ref, kv1_ref)):
        for s in range(CMP_STRIDE):
            x = kv_ref[pl.ds(s, nch, stride=CMP_STRIDE), :]
            x_ref[g * nch:(g + 1) * nch, s * HEAD_DIM:(s + 1) * HEAD_DIM] = x.astype(BF16)
    part = _dot(x_ref[...], wc_ref[...])
    for g in range(G_B):
        o_ref[g] = _cmp_finish(part[g * nch:(g + 1) * nch], cb_ref[...], w2_ref[...])


def _cmp_prompt(kv, wc, w2, cb, B, T):
    nch = T // CMP_STRIDE
    assert nch % 16 == 0
    return pl.pallas_call(
        functools.partial(_cmp_prompt_kernel, nch=nch),
        grid=(B, 2),
        in_specs=[pl.BlockSpec((T, HEAD_DIM), lambda b, r: (b, G_B * r)),
                  pl.BlockSpec((T, HEAD_DIM), lambda b, r: (b, G_B * r + 1)),
                  pl.BlockSpec((None,) + wc.shape[1:], lambda b, r: (r, 0, 0)),
                  pl.BlockSpec((None, HEAD_DIM, HEAD_DIM), lambda b, r: (r, 0, 0)),
                  pl.BlockSpec((None, 16, wc.shape[2]), lambda b, r: (r, 0, 0))],
        out_specs=pl.BlockSpec((None, None, G_B, nch, HEAD_DIM), lambda b, r: (b, r, 0, 0, 0)),
        out_shape=jax.ShapeDtypeStruct((B, 2, G_B, nch, HEAD_DIM), F32),
        scratch_shapes=[pltpu.VMEM((G_B * nch, CMP_STRIDE * HEAD_DIM), BF16)],
        compiler_params=_cparams(("parallel", "parallel")),
    )(kv, kv, wc, w2, cb)


def _stack_heads(q, hpg):
    return jnp.concatenate([q[:, r * HEAD_DIM:(r + 1) * HEAD_DIM] for r in range(hpg)],
                           axis=0).astype(BF16)


def _softmax_pieces(pieces, hpg, tq, slopes):
    outs, probs = [], [[] for _ in pieces]
    for r in range(hpg):
        ss = []
        for (s, dist, mask, _) in pieces:
            sr = s[r * tq:(r + 1) * tq] - slopes[r] * dist
            ss.append(jnp.where(mask, sr, NEG))
        m = ss[0].max(axis=-1, keepdims=True)
        for sr in ss[1:]:
            m = jnp.maximum(m, sr.max(axis=-1, keepdims=True))
        es = [jnp.where(mask, jnp.exp(sr - m), 0.0) for sr, (_, _, mask, _) in zip(ss, pieces)]
        l = es[0].sum(axis=-1, keepdims=True)
        for ee in es[1:]:
            l = l + ee.sum(axis=-1, keepdims=True)
        inv = 1.0 / jnp.maximum(l, 1e-30)
        for i, ee in enumerate(es):
            probs[i].append(ee * inv)
    probs = [jnp.concatenate(p, axis=0) for p in probs]
    o = _dot(probs[0].astype(BF16), pieces[0][3])
    for p, piece in zip(probs[1:], pieces[1:]):
        o = o + _dot(p.astype(BF16), piece[3])
    return probs, o


def _sel_score_mask(imp, msel, qpos, nsel, ns):
    hi = imp.astype(BF16).astype(F32)
    mid = (imp - hi).astype(BF16).astype(F32)
    lo = imp - hi - mid
    p = _dot_fewrows(hi, msel) + _dot_fewrows(mid, msel) + _dot_fewrows(lo, msel)
    t, nsp = p.shape
    if t == nsp:
        nsr = -(-ns // SUBLANES) * SUBLANES
        p = p.T[:nsr]
        j = lax.broadcasted_iota(jnp.int32, (nsr, 1), 0)
        cur = (qpos[0:1] + lax.broadcasted_iota(jnp.int32, (1, t), 1)) // SEL_BLOCK
    else:
        j = lax.broadcasted_iota(jnp.int32, (1, nsp), 1)
        cur = qpos // SEL_BLOCK
    valid = (j <= cur) & (j < ns)
    forced = (j == 0) | (j == cur) | (j == cur - 1)
    score = jnp.where(valid, p + jnp.where(forced, FORCE_BONUS, 0.0), NEG)
    cnt = jnp.zeros(score.shape, F32)
    for i in range(ns):
        other = score[i:i + 1] if t == nsp else score[:, i:i + 1]
        beats = (other > score) | ((other == score) & (i < j))
        cnt = cnt + jnp.where(beats, 1.0, 0.0)
    sel = jnp.where((cnt < nsel) & valid, 1.0, 0.0)
    if t == nsp:
        sel = jnp.concatenate([sel, jnp.zeros((nsp - nsr, t), F32)], axis=0).T if nsr < nsp else sel.T
    return sel


def _nsa_prompt_kernel(sl_ref, q_ref, gt_ref, kc_ref, vc_ref, ks_ref, vs_ref, kw_ref, vw_ref,
                       msel_ref, paug_ref, srow_ref, o_ref, *, tq, tk, hpg, nsel, ns, T):
    g = pl.program_id(1)
    qi = pl.program_id(2)
    q0 = qi * tq
    slopes = [sl_ref[g * hpg + r] for r in range(hpg)]
    qpos = q0 + lax.broadcasted_iota(jnp.int32, (tq, 1), 0)
    lane = lax.broadcasted_iota(jnp.int32, (1, HEAD_DIM), 1)
    Q = _stack_heads(q_ref[...] * SCALE, hpg)

    ncp = kc_ref.shape[0]
    end = lax.broadcasted_iota(jnp.int32, (1, ncp), 1) * CMP_STRIDE + (CMP_BLOCK - 1)
    dist = qpos - end
    sc = _dot_nt(Q, kc_ref[...].astype(BF16))
    (pc,), o_cmp = _softmax_pieces([(sc, dist.astype(F32), dist >= 0, vc_ref[...].astype(BF16))],
                                   hpg, tq, slopes)
    imp = pc[0:tq]
    for r in range(1, hpg):
        imp = imp + pc[r * tq:(r + 1) * tq]
    selm = _sel_score_mask(imp, msel_ref[...], qpos, nsel, ns)

    srows = [srow_ref[pl.ds(g * hpg + r, 1), :] for r in range(hpg)]
    selneg = jnp.where((selm > 0.5) | (lane >= ns), 0.0, MASK_BF16)
    q_sel = jnp.concatenate(
        [Q, jnp.concatenate([selneg + sr for sr in srows], axis=0).astype(BF16)], axis=1)
    q_win = jnp.concatenate(
        [Q, jnp.concatenate([jnp.broadcast_to(sr, (tq, HEAD_DIM)) for sr in srows], axis=0).astype(BF16)],
        axis=1)
    ones_col = jnp.where(lane == 0, 1.0, 0.0).astype(BF16)

    def attend_tile(q_aug, k_ref, v_ref, k0, n, bias, ms, acc):
        k_aug = jnp.concatenate([k_ref[pl.ds(k0, n), :].astype(BF16), paug_ref[pl.ds(k0, n), :]], axis=1)
        v_aug = jnp.concatenate([v_ref[pl.ds(k0, n), :].astype(BF16),
                                 jnp.broadcast_to(ones_col, (n, HEAD_DIM))], axis=1)
        s = _dot_nt(q_aug, k_aug)
        new_ms, ps, alphas = [], [], []
        for r in range(hpg):
            sr = s[r * tq:(r + 1) * tq]
            if bias is not None:
                sr = sr + bias
            m_new = jnp.maximum(ms[r], sr.max(axis=-1, keepdims=True))
            new_ms.append(m_new)
            ps.append(jnp.exp(sr - m_new))
            alphas.append(jnp.broadcast_to(jnp.exp(ms[r] - m_new), (tq, 2 * HEAD_DIM)))
        pv = _dot(jnp.concatenate(ps, axis=0).astype(BF16), v_aug)
        return tuple(new_ms), jnp.concatenate(alphas, axis=0) * acc + pv

    def finish(acc):
        return acc[:, :HEAD_DIM] / jnp.maximum(acc[:, HEAD_DIM:HEAD_DIM + 1], 1e-30)

    init = (tuple(jnp.full((tq, 1), NEG, F32) for _ in range(hpg)),
            jnp.zeros((hpg * tq, 2 * HEAD_DIM), F32))

    nkt = (q0 + tq + tk - 1) // tk

    def kt_body(kt, carry):
        return attend_tile(q_sel, ks_ref, vs_ref, pl.multiple_of(kt * tk, tk), tk, None, *carry)

    ms, acc = lax.fori_loop(0, nkt - 1, kt_body, init)
    k0 = pl.multiple_of((nkt - 1) * tk, tk)
    causal = jnp.where(qpos >= k0 + lax.broadcasted_iota(jnp.int32, (1, tk), 1), 0.0, NEG)
    _, acc = attend_tile(q_sel, ks_ref, vs_ref, k0, tk, causal, ms, acc)
    o_sel = finish(acc)

    wl = WINDOW + tq
    kstart = pl.multiple_of(jnp.maximum(q0 - WINDOW, 0), tq)
    dw = qpos - (kstart + lax.broadcasted_iota(jnp.int32, (1, wl), 1))
    band = jnp.where((dw >= 0) & (dw < WINDOW), 0.0, NEG)
    _, acc = attend_tile(q_win, kw_ref, vw_ref, kstart, wl, band, *init)
    o_win = finish(acc)

    gt = _sigmoid(gt_ref[...])
    for r in range(hpg):
        rows = slice(r * tq, (r + 1) * tq)
        o_ref[:, r * HEAD_DIM:(r + 1) * HEAD_DIM] = (
            gt[:, 3 * r:3 * r + 1] * o_cmp[rows] + gt[:, 3 * r + 1:3 * r + 2] * o_sel[rows]
            + gt[:, 3 * r + 2:3 * r + 3] * o_win[rows])


def _alibi_slopes_np(h_b):
    h = np.arange(1, h_b + 1, dtype=np.float32)
    return np.exp2(-8.0 * h / h_b).astype(np.float32)


def _alibi_slopes(h_b):
    return jnp.asarray(_alibi_slopes_np(h_b))


def _msel_matrix(ncp, nsp, nc, ns):
    i = np.arange(ncp)[:, None]
    j = np.arange(nsp)[None, :]
    r = SEL_BLOCK // CMP_STRIDE
    back = CMP_BLOCK // CMP_STRIDE - 1
    m = (i >= r * j - back) & (i <= r * j + r - 1) & (i < nc) & (j < ns)
    return jnp.asarray(m.astype(np.float32), dtype=BF16)


def _expand_matrix(nsteps, nsp, tk):
    s = np.arange(nsteps)[:, None, None]
    j = np.arange(nsp)[None, :, None]
    key = np.arange(tk)[None, None, :]
    return jnp.asarray(((s * tk + key) // SEL_BLOCK == j).astype(np.float32), dtype=BF16)


def _nsa_prompt(u, kv, kcv, B, T, hpg):
    tq, tk = 128, 512
    assert T % tk == 0 and T >= WINDOW + tq
    ncp = T // CMP_STRIDE
    nc = ncp - CMP_BLOCK // CMP_STRIDE + 1
    ns = -(-T // SEL_BLOCK)
    nsp = HEAD_DIM
    assert ns <= POS_LANE and T <= 256 * HEAD_DIM
    nsel = min(N_SEL, ns)
    nqt = T // tq
    qw = hpg * HEAD_DIM
    gate_block0 = (G_B * qw + MEM_WIDTH) // HEAD_DIM
    msel = _msel_matrix(ncp, nsp, nc, ns)
    kpos = np.arange(T)
    pa = np.zeros((T, HEAD_DIM), np.float32)
    pa[kpos, kpos // SEL_BLOCK] = 1.0
    pa[:, POS_LANE:POS_LANE + 3] = (kpos // HEAD_DIM * HEAD_DIM)[:, None]
    pa[:, POS_LANE + 3:POS_LANE + 6] = (kpos % HEAD_DIM)[:, None]
    paug = jnp.asarray(pa, dtype=BF16)
    sl = np.asarray(_alibi_slopes_np(G_B * hpg))
    s1 = sl.astype(ml_dtypes.bfloat16).astype(np.float32)
    s2 = (sl - s1).astype(ml_dtypes.bfloat16).astype(np.float32)
    s3 = (sl - s1 - s2).astype(ml_dtypes.bfloat16).astype(np.float32)
    sr = np.zeros((-(-G_B * hpg // SUBLANES) * SUBLANES, HEAD_DIM), np.float32)
    for c, piece in enumerate((s1, s2, s3, s1, s2, s3)):
        sr[:G_B * hpg, POS_LANE + c] = piece
    srow = jnp.asarray(sr)

    def kvcol(c):
        return pl.BlockSpec((T, HEAD_DIM), lambda b, g, t: (b, c + g))

    def cmpspec(r):
        return pl.BlockSpec((None, None, None, ncp, HEAD_DIM), lambda b, g, t: (b, r, g, 0, 0))

    return pl.pallas_call(
        functools.partial(_nsa_prompt_kernel, tq=tq, tk=tk, hpg=hpg, nsel=nsel, ns=ns, T=T),
        grid=(B, G_B, nqt),
        in_specs=[pl.BlockSpec(memory_space=pltpu.SMEM),
                  pl.BlockSpec((tq, qw), lambda b, g, t: (b * nqt + t, g)),
                  pl.BlockSpec((tq, HEAD_DIM), lambda b, g, t: (b * nqt + t, gate_block0 + g)),
                  cmpspec(0), cmpspec(1), kvcol(4), kvcol(6), kvcol(8), kvcol(10),
                  pl.BlockSpec(msel.shape, lambda b, g, t: (0, 0)),
                  pl.BlockSpec(paug.shape, lambda b, g, t: (0, 0)),
                  pl.BlockSpec(srow.shape, lambda b, g, t: (0, 0))],
        out_specs=pl.BlockSpec((tq, qw), lambda b, g, t: (b * nqt + t, g)),
        out_shape=jax.ShapeDtypeStruct((B * T, G_B * qw), F32),
        compiler_params=_cparams(("parallel", "parallel", "arbitrary")),
    )(_alibi_slopes(G_B * hpg), u, u, kcv, kcv, kv, kv, kv, kv, msel, paug, srow)


CMP_PAGES = 16
SEL_PAGES = 8


def _cmp_pages_kernel(pt_ref, *refs, npage):
    page_refs = refs[:npage]
    wc_ref, o_ref, x_ref = refs[npage:]
    cpp = PAGE_SIZE // CMP_STRIDE
    rows = npage * cpp
    slabs = page_refs[0].shape[0] // PAGE_SIZE
    for rt in range(2):
        for g in range(G_B):
            for p in range(npage):
                for s in range(CMP_STRIDE):
                    x_ref[rt, g * rows + p * cpp:g * rows + (p + 1) * cpp, s * HEAD_DIM:(s + 1) * HEAD_DIM] = (
                        page_refs[p][pl.ds(s * slabs + rt * G_B + g, cpp, stride=CMP_STRIDE * slabs), :])
    for rt in range(2):
        part = _dot(x_ref[rt].astype(BF16), wc_ref[rt])
        for g in range(G_B):
            o_ref[rt, g] = part[g * rows:(g + 1) * rows]


def _cmp_pages(cache, page_table, wc):
    Bs, npg = page_table.shape
    npage = CMP_PAGES
    assert npg % npage == 0
    cpp = PAGE_SIZE // CMP_STRIDE
    rows = npage * cpp
    nch = npg * cpp

    def page_spec(p):
        return pl.BlockSpec((None,) + cache.shape[1:], lambda b, s, pt: (pt[b, s * npage + p], 0, 0))

    grid_spec = pltpu.PrefetchScalarGridSpec(
        num_scalar_prefetch=1,
        grid=(Bs, npg // npage),
        in_specs=[page_spec(p) for p in range(npage)]
        + [pl.BlockSpec(wc.shape, lambda b, s, pt: (0, 0, 0))],
        out_specs=pl.BlockSpec((None, 2, G_B, rows, wc.shape[2]), lambda b, s, pt: (b, 0, 0, s, 0)),
        scratch_shapes=[pltpu.VMEM((2, G_B * rows, CMP_STRIDE * HEAD_DIM), F32)],
    )
    return pl.pallas_call(
        functools.partial(_cmp_pages_kernel, npage=npage),
        grid_spec=grid_spec,
        out_shape=jax.ShapeDtypeStruct((Bs, 2, G_B, nch, wc.shape[2]), F32),
        compiler_params=_cparams(("parallel", "arbitrary")),
    )(page_table, *([cache] * npage), wc)


def _nsa_sample_cmp_kernel(sl_ref, part_ref, cb_ref, w2_ref, q_ref, msel_ref, ocmp_ref, selm_ref,
                           *, tp, hpg, nsel, ns, past):
    g = pl.program_id(1)
    slopes = [sl_ref[g * hpg + r] for r in range(hpg)]
    kc = _cmp_finish(part_ref[0], cb_ref[0], w2_ref[0]).astype(BF16)
    vc = _cmp_finish(part_ref[1], cb_ref[1], w2_ref[1]).astype(BF16)
    nch = kc.shape[0]
    qpos = past + lax.broadcasted_iota(jnp.int32, (tp, 1), 0)
    Q = _stack_heads(q_ref[...], hpg)
    idx = lax.broadcasted_iota(jnp.int32, (1, nch), 1)
    dist = qpos - (idx * CMP_STRIDE + (CMP_BLOCK - 1))
    mask = (dist >= 0) & (idx < nch - 1)
    sc = _dot_nt(Q, kc) * SCALE
    (pc,), o_cmp = _softmax_pieces([(sc, dist.astype(F32), mask, vc)], hpg, tp, slopes)
    imp = pc[0:tp]
    for r in range(1, hpg):
        imp = imp + pc[r * tp:(r + 1) * tp]
    ocmp_ref[...] = o_cmp
    selm_ref[...] = _sel_score_mask(imp, msel_ref[...], qpos, nsel, ns)


def _nsa_sample_cmp(u, part, cb, w2, Bs, tp, hpg, past):
    nch = part.shape[3]
    nc = nch - CMP_BLOCK // CMP_STRIDE + 1
    ns = past // SEL_BLOCK + 1
    nsp = -(-ns // 128) * 128
    nsel = min(N_SEL, ns)
    qw = hpg * HEAD_DIM
    msel = _msel_matrix(nch, nsp, nc, ns)
    return pl.pallas_call(
        functools.partial(_nsa_sample_cmp_kernel, tp=tp, hpg=hpg, nsel=nsel, ns=ns, past=past),
        grid=(Bs, G_B),
        in_specs=[pl.BlockSpec(memory_space=pltpu.SMEM),
                  pl.BlockSpec((None, 2, None, nch, part.shape[4]), lambda b, g: (b, 0, g, 0, 0)),
                  pl.BlockSpec(cb.shape, lambda b, g: (0, 0, 0)),
                  pl.BlockSpec(w2.shape, lambda b, g: (0, 0, 0)),
                  pl.BlockSpec((None, tp, qw), lambda b, g: (b, 0, g)),
                  pl.BlockSpec(msel.shape, lambda b, g: (0, 0))],
        out_specs=[pl.BlockSpec((None, None, hpg * tp, HEAD_DIM), lambda b, g: (b, g, 0, 0)),
                   pl.BlockSpec((None, None, tp, nsp), lambda b, g: (b, g, 0, 0))],
        out_shape=[jax.ShapeDtypeStruct((Bs, G_B, hpg * tp, HEAD_DIM), F32),
                   jax.ShapeDtypeStruct((Bs, G_B, tp, nsp), F32)],
        compiler_params=_cparams(("parallel", "parallel")),
    )(_alibi_slopes(G_B * hpg), part, cb, w2, u, msel)


def _nsa_sample_kernel(pt_ref, sl_ref, *refs, npage, tp, hpg, past):
    page_refs = refs[:npage]
    q_ref, gt_ref, selm_ref, ocmp_ref, kvn_ref, cwin_ref, e_ref, o_ref, m_ref, l_ref, acc_ref = refs[npage:]
    s_id = pl.program_id(1)
    nsteps = pl.num_programs(1)
    tk = npage * PAGE_SIZE
    rows = hpg * tp
    qpos = past + lax.broadcasted_iota(jnp.int32, (tp, 1), 0)

    @pl.when(s_id == 0)
    def _():
        m_ref[...] = jnp.full(m_ref.shape, NEG, F32)
        l_ref[...] = jnp.zeros(l_ref.shape, F32)
        acc_ref[...] = jnp.zeros(acc_ref.shape, F32)

    kpos = s_id * tk + lax.broadcasted_iota(jnp.int32, (1, tk), 1)
    d = qpos - kpos
    df = d.astype(F32)
    for g in range(G_B):
        slopes = [sl_ref[g * hpg + r] for r in range(hpg)]
        Q = _stack_heads(q_ref[:, g * hpg * HEAD_DIM:(g + 1) * hpg * HEAD_DIM], hpg)
        slabs = page_refs[0].shape[0] // PAGE_SIZE
        kb = jnp.concatenate([page_refs[p][pl.ds(2 * G_B + g, PAGE_SIZE, stride=slabs), :]
                              for p in range(npage)], axis=0).astype(BF16)
        vb = jnp.concatenate([page_refs[p][pl.ds(3 * G_B + g, PAGE_SIZE, stride=slabs), :]
                              for p in range(npage)], axis=0).astype(BF16)
        s = _dot_nt(Q, kb) * SCALE
        mask = (_dot_fewrows(selm_ref[g], e_ref[...]) > 0.5) & (d >= 0)
        ps, alphas = [], []
        for r in range(hpg):
            hr = slice(r * tp, (r + 1) * tp)
            sr = jnp.where(mask, s[hr] - slopes[r] * df, NEG)
            m_old = m_ref[g, hr]
            m_new = jnp.maximum(m_old, sr.max(axis=-1, keepdims=True))
            alpha = jnp.exp(m_old - m_new)
            p = jnp.where(mask, jnp.exp(sr - m_new), 0.0)
            m_ref[g, hr] = m_new
            l_ref[g, hr] = alpha * l_ref[g, hr] + p.sum(axis=-1, keepdims=True)
            ps.append(p)
            alphas.append(jnp.broadcast_to(alpha, (tp, HEAD_DIM)))
        pv = _dot(jnp.concatenate(ps, axis=0).astype(BF16), vb)
        acc_ref[g] = jnp.concatenate(alphas, axis=0) * acc_ref[g] + pv

    @pl.when(s_id == nsteps - 1)
    def _():
        gt = _sigmoid(gt_ref[...])
        npad = HEAD_DIM
        kidx = lax.broadcasted_iota(jnp.int32, (1, npad), 1)
        dn = qpos - (past + kidx)
        new_ok = (kidx < tp) & (dn >= 0)
        zpad = jnp.zeros((npad - tp, HEAD_DIM), F32)
        nblk = past // SEL_BLOCK
        wpast = cwin_ref.shape[0] // (2 * G_B)
        dwp = qpos - (past - wpast + lax.broadcasted_iota(jnp.int32, (1, wpast), 1))
        for g in range(G_B):
            slopes = [sl_ref[g * hpg + r] for r in range(hpg)]
            Q = _stack_heads(q_ref[:, g * hpg * HEAD_DIM:(g + 1) * hpg * HEAD_DIM], hpg)

            def newrows(c):
                blk = kvn_ref[:, (c * G_B + g) * HEAD_DIM:(c * G_B + g + 1) * HEAD_DIM]
                return jnp.concatenate([blk, zpad], axis=0).astype(BF16)

            kn, vn = newrows(2), newrows(3)
            sn = _dot_nt(Q, kn) * SCALE
            mask_n = new_ok & (selm_ref[g][:, nblk:nblk + 1] > 0.5)
            o_sel = []
            for r in range(hpg):
                hr = slice(r * tp, (r + 1) * tp)
                sr = jnp.where(mask_n, sn[hr] - slopes[r] * dn.astype(F32), NEG)
                m_old = m_ref[g, hr]
                m_new = jnp.maximum(m_old, sr.max(axis=-1, keepdims=True))
                alpha = jnp.exp(m_old - m_new)
                p = jnp.where(mask_n, jnp.exp(sr - m_new), 0.0)
                l = alpha * l_ref[g, hr] + p.sum(axis=-1, keepdims=True)
                acc = alpha * acc_ref[g, hr] + _dot_fewrows(p, vn)
                o_sel.append(acc / jnp.maximum(l, 1e-30))
            kwp = cwin_ref[pl.ds(g, wpast, stride=2 * G_B), :].astype(BF16)
            vwp = cwin_ref[pl.ds(G_B + g, wpast, stride=2 * G_B), :].astype(BF16)
            kwn, vwn = newrows(4), newrows(5)
            pieces = [(_dot_nt(Q, kwp) * SCALE, dwp.astype(F32), (dwp >= 0) & (dwp < WINDOW), vwp),
                      (_dot_nt(Q, kwn) * SCALE, dn.astype(F32), new_ok & (dn < WINDOW), vwn)]
            _, o_win = _softmax_pieces(pieces, hpg, tp, slopes)
            o_cmp = ocmp_ref[g]
            for r in range(hpg):
                hr = slice(r * tp, (r + 1) * tp)
                c = 3 * (g * hpg + r)
                col = (g * hpg + r) * HEAD_DIM
                o_ref[:, col:col + HEAD_DIM] = (gt[:, c:c + 1] * o_cmp[hr] + gt[:, c + 1:c + 2] * o_sel[r]
                                                + gt[:, c + 2:c + 3] * o_win[hr])


def _nsa_sample(u, cache, page_table, selm, o_cmp, kvn, cwin, Bs, tp, hpg, past):
    npg = page_table.shape[1]
    npage = SEL_PAGES
    assert npg % npage == 0
    nsteps = npg // npage
    tk = npage * PAGE_SIZE
    nsp = selm.shape[3]
    qw = G_B * hpg * HEAD_DIM
    e3 = _expand_matrix(nsteps, nsp, tk)
    gates = jnp.concatenate([u[:, :, qw + MEM_WIDTH + g * HEAD_DIM:qw + MEM_WIDTH + g * HEAD_DIM + 3 * hpg]
                             for g in range(G_B)]
                            + [jnp.zeros((Bs, tp, HEAD_DIM - 3 * hpg * G_B), F32)], axis=2)

    def page_spec(p):
        return pl.BlockSpec((None,) + cache.shape[1:], lambda b, s, pt: (pt[b, s * npage + p], 0, 0))

    grid_spec = pltpu.PrefetchScalarGridSpec(
        num_scalar_prefetch=1,
        grid=(Bs, nsteps),
        in_specs=[pl.BlockSpec(memory_space=pltpu.SMEM)]
        + [page_spec(p) for p in range(npage)]
        + [pl.BlockSpec((None, tp, qw), lambda b, s, pt: (b, 0, 0)),
           pl.BlockSpec((None, tp, HEAD_DIM), lambda b, s, pt: (b, 0, 0)),
           pl.BlockSpec((None, G_B, tp, nsp), lambda b, s, pt: (b, 0, 0, 0)),
           pl.BlockSpec((None, G_B, hpg * tp, HEAD_DIM), lambda b, s, pt: (b, 0, 0, 0)),
           pl.BlockSpec((None, tp, kvn.shape[2]), lambda b, s, pt: (b, 0, 0)),
           pl.BlockSpec((None,) + cwin.shape[1:], lambda b, s, pt: (b, 0, 0)),
           pl.BlockSpec((None, nsp, tk), lambda b, s, pt: (s, 0, 0))],
        out_specs=pl.BlockSpec((None, tp, qw), lambda b, s, pt: (b, 0, 0)),
        scratch_shapes=[pltpu.VMEM((G_B, hpg * tp, 1), F32), pltpu.VMEM((G_B, hpg * tp, 1), F32),
                        pltpu.VMEM((G_B, hpg * tp, HEAD_DIM), F32)],
    )
    return pl.pallas_call(
        functools.partial(_nsa_sample_kernel, npage=npage, tp=tp, hpg=hpg, past=past),
        grid_spec=grid_spec,
        out_shape=jax.ShapeDtypeStruct((Bs, tp, qw), F32),
        compiler_params=_cparams(("parallel", "arbitrary")),
    )(page_table, _alibi_slopes(G_B * hpg), *([cache] * npage), u, gates, selm, o_cmp, kvn, cwin, e3)


def _prep_weights(w_in_a, w_in_b, w_o, w_mem_kv, w_kv_b, w_cmp1, w_cmp2, w_ffn_in, w_ffn_out, cmp_pos, hpg):
    tokw = w_in_b.shape[2] - 3 * G_B * hpg - MEM_WIDTH
    D = w_in_b.shape[1]
    wb = w_in_b[0]
    gates = wb[:, tokw:tokw + 3 * G_B * hpg]
    gpad = jnp.zeros((D, HEAD_DIM - 3 * hpg), F32)
    wb2 = jnp.concatenate([wb[:, :tokw], wb[:, tokw + 3 * G_B * hpg:]]
                          + sum([[gates[:, g * 3 * hpg:(g + 1) * 3 * hpg], gpad] for g in range(G_B)], []),
                          axis=1)
    r = CMP_BLOCK // CMP_STRIDE
    flat = CMP_STRIDE * HEAD_DIM
    wc = w_cmp1.reshape(2, r, flat, HEAD_DIM).transpose(0, 2, 1, 3).reshape(2, flat, r * HEAD_DIM)
    pe2 = jnp.concatenate([cmp_pos.reshape(2, r, flat), jnp.zeros((2, 16 - r, flat), F32)], axis=1)
    def per_layer(w):
        return [w[l].astype(BF16) for l in range(w.shape[0])]

    return dict(w_in_a=per_layer(w_in_a), w_in_b=wb2.astype(BF16), w_o=per_layer(w_o),
                w_mem_kv=per_layer(w_mem_kv), w_kv_b=w_kv_b.astype(BF16), wc=wc.astype(BF16),
                w2=w_cmp2.astype(BF16), w_ffn_in=per_layer(w_ffn_in), w_ffn_out=per_layer(w_ffn_out),
                pe2=pe2)


def _layer_tail(h, mix, mo, l, prm, conv_buf, B, T):
    g = prm['norm_gains'][l]
    h = _mmres(mix, mo, prm['w_o'][l], h, g[1])
    y, new_buf = _ffn_in(h, g[2], prm['w_ffn_in'][l], prm['w_ffn_conv'][l], prm['b_ffn_conv'][l],
                         None if conv_buf is None else conv_buf[l], B, T)
    h = _mmres(y, None, prm['w_ffn_out'][l], h, g[3])
    return h, new_buf


def _trunk(x, mem_kv, hgrn_s0, conv_buf, prm, B, T, t_valid, nsa_fn):
    D = x.shape[-1]
    H = prm['hgrn_norm'].shape[1]
    h = x.reshape(B * T, D)
    u = _mm(h, prm['norm_gains'][0, 0], prm['w_in_a'][0]).reshape(B, T, -1)
    mix, s_fin = _hgrn(u, prm['lb_logits'], prm['hgrn_norm'][0], None if hgrn_s0 is None else hgrn_s0[0],
                       B, T, H, t_valid)
    mo = _memattn(u, 4 * H * HEAD_DIM // MEM_WIDTH, mem_kv, 0, B, T)
    h, buf0 = _layer_tail(h, mix.reshape(B * T, -1), mo.reshape(B * T, -1), 0, prm, conv_buf, B, T)
    n_slab = prm['w_kv_b'].shape[1] // HEAD_DIM
    kv, kv_rows, kv_win = _mm(h, prm['kv_norm'], prm['w_kv_b'], slabs=(n_slab - 2 * G_B, 2 * G_B))
    u = _mm(h, prm['norm_gains'][1, 0], prm['w_in_b'])
    mix = nsa_fn(u, kv)
    mo = _memattn(u.reshape(B, T, -1), H * HEAD_DIM // MEM_WIDTH, mem_kv, 1, B, T)
    h, buf1 = _layer_tail(h, mix.reshape(B * T, -1), mo.reshape(B * T, -1), 1, prm, conv_buf, B, T)
    return h.reshape(B, T, D), s_fin, (buf0, buf1), kv_rows, kv_win


def kernel(x_prompt, x_sample, mem_prompt, state_hgrn, cache_conv, cache_mem, cache_kv, cache_win,
           page_table, norm_gains, w_in_a, lb_logits, hgrn_norm, w_in_b, w_o, w_mem_kv, kv_norm,
           w_kv_b, cmp_pos, w_cmp1, w_cmp2, w_ffn_in, w_ffn_conv, b_ffn_conv, w_ffn_out):
    Bp, Tp, D = x_prompt.shape
    Bs, Ts, _ = x_sample.shape
    depth = norm_gains.shape[0]
    assert depth == 2 and w_in_a.shape[0] == 1 and w_in_b.shape[0] == 1
    H = hgrn_norm.shape[1]
    hpg = H // G_B
    ml = mem_prompt.shape[1]
    n_rows = cache_kv.shape[2]

    prm = _prep_weights(w_in_a, w_in_b, w_o, w_mem_kv, w_kv_b, w_cmp1, w_cmp2, w_ffn_in, w_ffn_out,
                        cmp_pos, hpg)
    prm.update(norm_gains=norm_gains, lb_logits=lb_logits, hgrn_norm=hgrn_norm, kv_norm=kv_norm,
               w_ffn_conv=w_ffn_conv, b_ffn_conv=b_ffn_conv)
    cb = _cmp_bias(prm['pe2'], prm['wc'])

    memx = mem_prompt.reshape(Bp * ml, D)
    mem_kv_p = [_mm(memx, None, prm['w_mem_kv'][l], norm=False).reshape(Bp, ml, 2 * MEM_WIDTH)
                for l in range(depth)]

    def nsa_p(u, kv):
        kcv = _cmp_prompt(kv, prm['wc'], prm['w2'], cb, Bp, Tp)
        return _nsa_prompt(u, kv, kcv, Bp, Tp, hpg)

    y_p, hgrn_p, conv_p, kvr_p, kvw_p = _trunk(x_prompt, mem_kv_p, None, None, prm, Bp, Tp, Tp, nsa_p)

    tp = -(-Ts // SUBLANES) * SUBLANES
    past = page_table.shape[1] * PAGE_SIZE
    assert Ts < CMP_STRIDE and tp <= SEL_BLOCK and past >= WINDOW
    xs = jnp.concatenate([x_sample, jnp.zeros((Bs, tp - Ts, D), F32)], axis=1)

    assert n_rows == 4
    cache5 = cache_kv.reshape(cache_kv.shape[0], PAGE_SIZE * n_rows * G_B, HEAD_DIM)

    def nsa_s(u, kv):
        u3 = u.reshape(Bs, tp, -1)
        part = _cmp_pages(cache5, page_table, prm['wc'])
        o_cmp, selm = _nsa_sample_cmp(u3, part, cb, prm['w2'], Bs, tp, hpg, past)
        return _nsa_sample(u3, cache5, page_table, selm, o_cmp, kv.reshape(Bs, tp, -1),
                           cache_win.reshape(Bs, -1, HEAD_DIM), Bs, tp, hpg, past)

    y_s, hgrn_s, conv_s, kvr_s, kvw_s = _trunk(xs, cache_mem.reshape(depth, Bs, -1, HEAD_DIM), state_hgrn, cache_conv,
                                       prm, Bs, tp, Ts, nsa_s)

    wl = min(WINDOW, Tp)
    return (
        y_p,
        y_s[:, :Ts],
        hgrn_p[None],
        hgrn_s[None],
        jnp.stack(conv_p),
        jnp.stack([c[:, Ts - (CONV_W - 1):Ts] for c in conv_s]),
        jnp.stack(mem_kv_p).reshape(depth, Bp, ml, 2, MEM_HEADS, HEAD_DIM),
        kvr_p.reshape(Bp, Tp // PAGE_SIZE, PAGE_SIZE, n_rows, G_B, HEAD_DIM),
        kvr_s.reshape(Bs, tp, n_rows, G_B, HEAD_DIM)[:, :Ts],
        kvw_p.reshape(Bp, Tp, 2, G_B, HEAD_DIM)[:, Tp - wl:],
        kvw_s.reshape(Bs, tp, 2, G_B, HEAD_DIM)[:, :Ts],
    )
```

```python
import functools
import math

import ml_dtypes
import numpy as np
import jax
import jax.numpy as jnp
from jax import lax
from jax.experimental import pallas as pl
from jax.experimental.pallas import tpu as pltpu

F32 = jnp.float32
BF16 = jnp.bfloat16

HEAD_DIM = 128
MEM_HEADS = 4
MEM_WIDTH = MEM_HEADS * HEAD_DIM
G_B = 2
HGRN_CHUNK = 32
CMP_BLOCK = 32
CMP_STRIDE = 16
SEL_BLOCK = 64
N_SEL = 16
WINDOW = 512
PAGE_SIZE = 128
FORCE_BONUS = 1e4
CONV_W = 3
EPS = 1e-6
NEG = -1e30
MASK_BF16 = -2.0 ** 100
POS_LANE = 120
SCALE = HEAD_DIM ** -0.5
SUBLANES = 8
HGRN_HEADS = 6
VMEM_LIMIT = 56 * 1024 * 1024


def _cparams(sem):
    return pltpu.CompilerParams(dimension_semantics=sem, vmem_limit_bytes=VMEM_LIMIT)


def _dot(a, b):
    return jnp.dot(a, b, preferred_element_type=F32)


def _dot_nt(a, b):
    return lax.dot_general(a, b, (((1,), (1,)), ((), ())), preferred_element_type=F32)


def _dot_tn(a, b):
    return lax.dot_general(a, b, (((0,), (0,)), ((), ())), preferred_element_type=F32)


BF16_ROWS = 16


def _pad_rows(x):
    n = x.shape[0]
    if n % BF16_ROWS == 0:
        return x
    return jnp.concatenate([x, jnp.zeros((BF16_ROWS - n % BF16_ROWS,) + x.shape[1:], x.dtype)], axis=0)


def _dot_fewrows(a, b):
    return _dot(_pad_rows(a).astype(BF16), b)[:a.shape[0]]


def _rms(x, g):
    return x * lax.rsqrt(jnp.mean(x * x, axis=-1, keepdims=True) + EPS) * g


def _sigmoid(x):
    return 0.5 * jnp.tanh(0.5 * x) + 0.5


def _gelu(x):
    return 0.5 * x * (1.0 + jnp.tanh(0.7978845608028654 * (x + 0.044715 * (x * x * x))))


def _layer_weight(w, block, index_map):
    if isinstance(w, tuple):
        w3, l = w
        return w3, pl.BlockSpec((None,) + block, lambda *a: (l,) + index_map(*a))
    return w, pl.BlockSpec(block, index_map)


def _wshape(w):
    return w[0].shape[1:] if isinstance(w, tuple) else w.shape


def _pick(n, cands):
    for c in cands:
        if n % c == 0:
            return c
    raise ValueError(f"no tile for {n} in {cands}")


def _mm_kernel(x_ref, g_ref, w_ref, o_ref, *rest, norm, rows, slabs):
    xn_ref = rest[-1]
    tm = x_ref.shape[0]
    j = pl.program_id(1)

    @pl.when(j == 0)
    def _():
        def body(c, carry):
            r = pl.multiple_of(c * rows, rows)
            x = x_ref[pl.ds(r, rows), :]
            if norm:
                x = _rms(x, g_ref[...])
            xn_ref[pl.ds(r, rows), :] = x.astype(BF16)
            return carry
        lax.fori_loop(0, tm // rows, body, 0)

    res = _dot(xn_ref[...], w_ref[...])
    o_ref[...] = res
    if slabs:
        per_step = res.shape[1] // HEAD_DIM
        first = 0
        for s_ref, nslab in zip(rest[:-1], slabs):
            steps = nslab // per_step

            @pl.when((j >= first) & (j < first + steps))
            def _(s_ref=s_ref, nslab=nslab, first=first):
                for c in range(per_step):
                    s_ref[pl.ds((j - first) * per_step + c, tm, stride=nslab), :] = (
                        res[:, c * HEAD_DIM:(c + 1) * HEAD_DIM])
            first += steps


def _mm(x, g, w, *, norm=True, slabs=()):
    M, K = x.shape
    N = _wshape(w)[1]
    tm = _pick(M, (1024, 512, 256, 128, 64, 32, 16))
    tn = _pick(N, (1664, 1152, 512, 256, 128)) if not slabs else _pick(N, (512, 256, 128))
    w, w_spec = _layer_weight(w, (K, tn), lambda i, j: (0, j))
    rows = min(tm, 128)
    if g is None:
        g = jnp.ones((K,), F32)
    out_specs = [pl.BlockSpec((tm, tn), lambda i, j: (i, j))]
    out_shape = [jax.ShapeDtypeStruct((M, N), F32)]
    for n in slabs:
        assert (n * HEAD_DIM) % tn == 0
        out_specs.append(pl.BlockSpec((tm * n, HEAD_DIM), lambda i, j: (i, 0)))
        out_shape.append(jax.ShapeDtypeStruct((M * n, HEAD_DIM), F32))
    assert sum(slabs) in (0, N // HEAD_DIM)
    out = pl.pallas_call(
        functools.partial(_mm_kernel, norm=norm, rows=rows, slabs=tuple(slabs)),
        grid=(M // tm, N // tn),
        in_specs=[pl.BlockSpec((tm, K), lambda i, j: (i, 0)),
                  pl.BlockSpec((1, K), lambda i, j: (0, 0)),
                  w_spec],
        out_specs=out_specs,
        out_shape=out_shape,
        scratch_shapes=[pltpu.VMEM((tm, K), BF16)],
        compiler_params=_cparams(("parallel", "arbitrary")),
    )(x, g.reshape(1, K), w)
    return out if slabs else out[0]


def _mmres_kernel(*refs, nk, two, nxt):
    refs = list(refs)
    a1_ref = refs.pop(0)
    a = a1_ref[...].astype(BF16)
    if two:
        a = jnp.concatenate([a, refs.pop(0)[...].astype(BF16)], axis=1)
    w_ref, h_ref, g_ref = refs[:3]
    gn_ref = refs[3] if nxt else None
    o_ref = refs[4] if nxt else refs[3]
    xn_ref = refs[5] if nxt else None
    k = pl.program_id(1)
    part = _dot(a, w_ref[...])

    def finish(acc):
        hn = h_ref[...] + _rms(acc, g_ref[...])
        o_ref[...] = hn
        if nxt:
            xn_ref[...] = _rms(hn, gn_ref[...]).astype(BF16)

    if nk == 1:
        finish(part)
    else:
        @pl.when(k == 0)
        def _():
            o_ref[...] = part

        @pl.when((k > 0) & (k < nk - 1))
        def _():
            o_ref[...] += part

        @pl.when(k == nk - 1)
        def _():
            finish(o_ref[...] + part)


def _mmres(a1, a2, w, h, g, g_next=None):
    M, N = h.shape
    K = _wshape(w)[0]
    two = a2 is not None
    nk = 1 if two else 2
    tk = K // nk
    assert K % nk == 0 and tk % 128 == 0
    tm = _pick(M, (512, 256, 128, 64, 32, 16))
    if two:
        assert a1.shape[1] + a2.shape[1] == K
        in_specs = [pl.BlockSpec((tm, a1.shape[1]), lambda i, k: (i, 0)),
                    pl.BlockSpec((tm, a2.shape[1]), lambda i, k: (i, 0))]
        args = [a1, a2]
    else:
        in_specs = [pl.BlockSpec((tm, tk), lambda i, k: (i, k))]
        args = [a1]
    w, w_spec = _layer_weight(w, (tk, N), lambda i, k: (k, 0))
    in_specs += [w_spec,
                 pl.BlockSpec((tm, N), lambda i, k: (i, 0)),
                 pl.BlockSpec((1, N), lambda i, k: (0, 0))]
    args += [w, h, g.reshape(1, N)]
    out_specs = [pl.BlockSpec((tm, N), lambda i, k: (i, 0))]
    out_shape = [jax.ShapeDtypeStruct((M, N), F32)]
    nxt = g_next is not None
    if nxt:
        in_specs.append(pl.BlockSpec((1, N), lambda i, k: (0, 0)))
        args.append(g_next.reshape(1, N))
        out_specs.append(pl.BlockSpec((tm, N), lambda i, k: (i, 0)))
        out_shape.append(jax.ShapeDtypeStruct((M, N), BF16))
    out = pl.pallas_call(
        functools.partial(_mmres_kernel, nk=nk, two=two, nxt=nxt),
        grid=(M // tm, nk),
        in_specs=in_specs,
        out_specs=out_specs,
        out_shape=out_shape,
        compiler_params=_cparams(("parallel", "arbitrary")),
    )(*args)
    return out if nxt else out[0]


def _ffn_in_kernel(xn_ref, wa_ref, wb_ref, wc_ref, bc_ref, p1_ref, p2_ref,
                   y_ref, a_ref, carry_ref, *, T, rparts, cparts):
    tm = xn_ref.shape[0]
    i = pl.program_id(0)
    j = pl.program_id(1)
    wc = wc_ref[...]
    bc = bc_ref[...]
    if T >= tm:
        tiles_per_batch = T // tm

        @pl.when(i % tiles_per_batch == 0)
        def _():
            carry_ref[j] = p1_ref[...]

        prev = carry_ref[j]
        pr, pc = tm // rparts, wa_ref.shape[1] // cparts
        row = lax.broadcasted_iota(jnp.int32, (pr, 1), 0)
        for ch in range(cparts):
            cs = slice(ch * pc, (ch + 1) * pc)
            older, newer = prev[SUBLANES - 2:SUBLANES - 1, cs], prev[SUBLANES - 1:SUBLANES, cs]
            for rh in range(rparts):
                rs = slice(rh * pr, (rh + 1) * pr)
                xn = xn_ref[rs, :]
                a = _dot(xn, wa_ref[:, cs])
                b = _dot(xn, wb_ref[:, cs])
                a1 = jnp.where(row == 0, newer, pltpu.roll(a, 1, 0))
                a2 = jnp.where(row == 0, older, jnp.where(row == 1, newer, pltpu.roll(a, 2, 0)))
                c = bc[:, cs] + a2 * wc[0:1, cs] + a1 * wc[1:2, cs] + a * wc[2:3, cs]
                y_ref[rs, cs] = (_gelu(c) * b).astype(BF16)
                older, newer = a[pr - 2:pr - 1], a[pr - 1:pr]
            carry_ref[j, :, cs] = a[pr - SUBLANES:pr]
            a_ref[:, cs] = a[pr - SUBLANES:pr]
    else:
        xn = xn_ref[...]
        a = _dot(xn, wa_ref[...])
        b = _dot(xn, wb_ref[...])
        tmod = lax.broadcasted_iota(jnp.int32, (tm, 1), 0) % T
        a1 = jnp.where(tmod == 0, p1_ref[...], pltpu.roll(a, 1, 0))
        a2 = jnp.where(tmod < 2, p2_ref[...], pltpu.roll(a, 2, 0))
        a_ref[...] = a
        c = bc + a2 * wc[0:1] + a1 * wc[1:2] + a * wc[2:3]
        y_ref[...] = (_gelu(c) * b).astype(BF16)


def _ffn_in(x, w, wconv, bconv, buf, B, T):
    M, K = x.shape
    F = _wshape(w)[1] // 2
    tn = 512
    nj = F // tn
    _, wa_spec = _layer_weight(w, (K, tn), lambda i, j: (0, j))
    w, wb_spec = _layer_weight(w, (K, tn), lambda i, j: (0, j + nj))
    assert F % tn == 0
    if buf is None:
        buf = jnp.zeros((B, CONV_W - 1, F), F32)
    wc = jnp.concatenate([wconv, jnp.zeros((SUBLANES - CONV_W, F), F32)], axis=0)
    if T >= 512:
        tm = _pick(T, (2048, 1024, 512))
        tpb = T // tm
        p1 = jnp.concatenate([jnp.zeros((B, SUBLANES - 2, F), F32), buf], axis=1)
        p2 = p1
        p_specs = [pl.BlockSpec((None, SUBLANES, tn), lambda i, j: (i // tpb, 0, j))] * 2
        a_shape = jax.ShapeDtypeStruct((M // tm, SUBLANES, F), F32)
        a_spec = pl.BlockSpec((None, SUBLANES, tn), lambda i, j: (i, 0, j))
    else:
        tm = M
        assert T >= 2 and M % T == 0
        z = jnp.zeros((B, 1, F), F32)
        p1 = jnp.concatenate([buf[:, 1:2]] + [z] * (T - 1), axis=1).reshape(M, F)
        p2 = jnp.concatenate([buf[:, 0:1], buf[:, 1:2]] + [z] * (T - 2), axis=1).reshape(M, F)
        p_specs = [pl.BlockSpec((tm, tn), lambda i, j: (i, j))] * 2
        a_shape = jax.ShapeDtypeStruct((M, F), F32)
        a_spec = pl.BlockSpec((tm, tn), lambda i, j: (i, j))
    y, a = pl.pallas_call(
        functools.partial(_ffn_in_kernel, T=T, rparts=max(tm // 1024, 1), cparts=2),
        grid=(M // tm, nj),
        in_specs=[pl.BlockSpec((tm, K), lambda i, j: (i, 0)),
                  wa_spec, wb_spec,
                  pl.BlockSpec((SUBLANES, tn), lambda i, j: (0, j)),
                  pl.BlockSpec((1, tn), lambda i, j: (0, j))] + p_specs,
        out_specs=[pl.BlockSpec((tm, tn), lambda i, j: (i, j)), a_spec],
        out_shape=[jax.ShapeDtypeStruct((M, F), BF16), a_shape],
        scratch_shapes=[pltpu.VMEM((nj, SUBLANES, tn), F32)],
        compiler_params=_cparams(("arbitrary", "arbitrary")),
    )(x, w, w, wc, bconv.reshape(1, F), p1, p2)
    if T >= 512:
        new_buf = a.reshape(B, T // tm, SUBLANES, F)[:, -1, SUBLANES - (CONV_W - 1):]
    else:
        new_buf = a.reshape(B, T, F)
    return y, new_buf


def _cumsum_tile(x):
    row = lax.broadcasted_iota(jnp.int32, (SUBLANES, 1), 0)
    d = 1
    while d < SUBLANES:
        x = x + jnp.where(row >= d, pltpu.roll(x, d, 0), 0.0)
        d *= 2
    return x


def _hgrn_chunk(uq, uf, v, ug, lb, gn, st, C, CP):
    nsub = CP // SUBLANES
    row = lax.broadcasted_iota(jnp.int32, (CP, 1), 0)
    row8 = row[0:SUBLANES]
    q = uq * _sigmoid(uq)
    fg = lb + (1.0 - lb) * _sigmoid(uf)
    kk = 1.0 - fg
    lf = jnp.log(fg)
    if C < CP:
        kk = jnp.where(row < C, kk, 0.0)
        lf = jnp.where(row < C, lf, 0.0)
    b_tiles, ends, off = [], [], None
    for i in range(nsub):
        bi = _cumsum_tile(lf[i * SUBLANES:(i + 1) * SUBLANES])
        if off is not None:
            bi = bi + off
        off = bi[SUBLANES - 1:SUBLANES]
        b_tiles.append(bi)
        ends.append(off)
    b = jnp.concatenate(b_tiles, axis=0) if nsub > 1 else b_tiles[0]
    b_last = off
    o = _dot_nt(_pad_rows(q * jnp.exp(b)).astype(BF16), st.astype(BF16))[:CP]
    if nsub > 1:
        e_all = jnp.concatenate([jnp.broadcast_to(e, (SUBLANES, HEAD_DIM)) for e in ends], axis=0)
        ke = kk * jnp.exp(e_all - b)
        lhs, rhs = [], []
        for j in range(nsub - 1):
            lo, hi = j * SUBLANES, (j + 1) * SUBLANES
            lhs.append(jnp.where(row >= hi, q * jnp.exp(jnp.minimum(b - ends[j], 0.0)), 0.0))
            rhs.append(jnp.where((row >= lo) & (row < hi), ke, 0.0))
        a_off = _dot_nt(jnp.concatenate(lhs, axis=1).astype(BF16), jnp.concatenate(rhs, axis=1).astype(BF16))
        o = o + _dot(a_off.astype(BF16), v.astype(BF16))
        kd = ke * jnp.exp(b_last - e_all)
    else:
        kd = kk * jnp.exp(b_last - b)
    o_tiles = []
    for i in range(nsub):
        sl = slice(i * SUBLANES, (i + 1) * SUBLANES)
        oi, qi, bi = o[sl], q[sl], b_tiles[i]
        for s in range(i * SUBLANES, min((i + 1) * SUBLANES, C)):
            d = jnp.exp(jnp.minimum(bi - b[s:s + 1], 0.0))
            a = jnp.sum(qi * kk[s:s + 1] * d, axis=-1, keepdims=True)
            oi = oi + jnp.where(row8 >= s - i * SUBLANES, a, 0.0) * v[s:s + 1]
        o_tiles.append(oi)
    o = jnp.concatenate(o_tiles, axis=0) if nsub > 1 else o_tiles[0]
    st_new = st * jnp.exp(b_last) + _dot_tn(_pad_rows(v).astype(BF16), _pad_rows(kd).astype(BF16))
    return _rms(o, gn) * _sigmoid(ug), st_new


def _hgrn_kernel(uq_ref, uf_ref, uv_ref, ug_ref, lbl_ref, gn_ref, s0_ref, o_ref, sfin_ref,
                 st_ref, *, C, CP, nchunk, hb):
    t = pl.program_id(2)

    @pl.when(t == 0)
    def _():
        for h in range(hb):
            st_ref[h] = s0_ref[h].T

    lbl = lbl_ref[...]
    e = jnp.exp(lbl - jnp.max(lbl, axis=0, keepdims=True))
    lb = e[0:1] / jnp.sum(e, axis=0, keepdims=True)
    gn = gn_ref[...]

    def chunk(c, carry):
        r0 = pl.multiple_of(c * CP, CP)
        for h in range(hb):
            cs = slice(h * HEAD_DIM, (h + 1) * HEAD_DIM)
            out, st_new = _hgrn_chunk(uq_ref[pl.ds(r0, CP), cs], uf_ref[pl.ds(r0, CP), cs],
                                      uv_ref[pl.ds(r0, CP), cs], ug_ref[pl.ds(r0, CP), cs],
                                      lb[:, cs], gn[:, cs], st_ref[h], C, CP)
            st_ref[h] = st_new
            o_ref[pl.ds(r0, CP), cs] = out
        return carry

    lax.fori_loop(0, nchunk, chunk, 0)

    @pl.when(t == pl.num_programs(2) - 1)
    def _():
        for h in range(hb):
            sfin_ref[h] = st_ref[h].T


def _hgrn(u, lb_logits, gnorm, s0, B, T, H, t_valid):
    C = math.gcd(t_valid, HGRN_CHUNK)
    assert C == t_valid or T == t_valid
    CP = max(C, SUBLANES)
    tt = _pick(T, (512, 256, 128, 64, 32, 16, 8))
    nchunk = tt // CP
    if s0 is None:
        s0 = jnp.zeros((B, H, HEAD_DIM, HEAD_DIM), F32)
    W = H * HEAD_DIM
    hb = _pick(H, (HGRN_HEADS, 2, 1))
    hg = H // hb
    wb = hb * HEAD_DIM

    def col(part):
        return pl.BlockSpec((None, tt, wb), lambda b, h, t: (b, t, part * hg + h))

    return pl.pallas_call(
        functools.partial(_hgrn_kernel, C=C, CP=CP, nchunk=nchunk, hb=hb),
        grid=(B, hg, T // tt),
        in_specs=[col(0), col(1), col(2), col(3),
                  pl.BlockSpec((lb_logits.shape[0], wb), lambda b, h, t: (0, h)),
                  pl.BlockSpec((1, wb), lambda b, h, t: (0, h)),
                  pl.BlockSpec((None, hb, HEAD_DIM, HEAD_DIM), lambda b, h, t: (b, h, 0, 0))],
        out_specs=[pl.BlockSpec((None, tt, wb), lambda b, h, t: (b, t, h)),
                   pl.BlockSpec((None, hb, HEAD_DIM, HEAD_DIM), lambda b, h, t: (b, h, 0, 0))],
        out_shape=[jax.ShapeDtypeStruct((B, T, W), F32),
                   jax.ShapeDtypeStruct((B, H, HEAD_DIM, HEAD_DIM), F32)],
        scratch_shapes=[pltpu.VMEM((hb, HEAD_DIM, HEAD_DIM), F32)],
        compiler_params=_cparams(("parallel", "parallel", "arbitrary")),
    )(u, u, u, u, lb_logits, gnorm.reshape(1, W), s0)


def _memattn_kernel(q_ref, kv_ref, o_ref):
    for h in range(MEM_HEADS):
        sl = slice(h * HEAD_DIM, (h + 1) * HEAD_DIM)
        q = _pad_rows(q_ref[:, sl]).astype(BF16)
        if kv_ref.shape[1] == HEAD_DIM:
            ml = kv_ref.shape[0] // (2 * MEM_HEADS)
            k = kv_ref[pl.ds(h, ml, stride=2 * MEM_HEADS), :].astype(BF16)
            v = kv_ref[pl.ds(MEM_HEADS + h, ml, stride=2 * MEM_HEADS), :].astype(BF16)
        else:
            k = kv_ref[:, sl].astype(BF16)
            v = kv_ref[:, MEM_WIDTH + h * HEAD_DIM:MEM_WIDTH + (h + 1) * HEAD_DIM].astype(BF16)
        s = _dot_nt(q, k) * SCALE
        p = jnp.exp(s - jnp.max(s, axis=-1, keepdims=True))
        l = jnp.sum(p, axis=-1, keepdims=True)
        o_ref[:, sl] = (_dot(p.astype(BF16), v) / l)[:q_ref.shape[0]]


def _memattn(u, qblock, mkv, layer, B, T):
    tq = _pick(T, (512, 256, 128, 64, 32, 16, 8))
    if isinstance(mkv, (list, tuple)):
        mkv = mkv[layer]
        kv_spec = pl.BlockSpec((None,) + mkv.shape[1:], lambda b, t: (b, 0, 0))
    else:
        kv_spec = pl.BlockSpec((None, None) + mkv.shape[2:], lambda b, t: (layer, b, 0, 0))
    return pl.pallas_call(
        _memattn_kernel,
        grid=(B, T // tq),
        in_specs=[pl.BlockSpec((None, tq, MEM_WIDTH), lambda b, t: (b, t, qblock)), kv_spec],
        out_specs=pl.BlockSpec((None, tq, MEM_WIDTH), lambda b, t: (b, t, 0)),
        out_shape=jax.ShapeDtypeStruct((B, T, MEM_WIDTH), F32),
        compiler_params=_cparams(("parallel", "parallel")),
    )(u, mkv)


def _cmp_bias_kernel(pe_ref, wc_ref, o_ref):
    o_ref[...] = _dot(pe_ref[...].astype(BF16), wc_ref[...])


def _cmp_bias(pe2, wc):
    return pl.pallas_call(
        _cmp_bias_kernel,
        grid=(2,),
        in_specs=[pl.BlockSpec((None, 16, pe2.shape[2]), lambda r: (r, 0, 0)),
                  pl.BlockSpec((None,) + wc.shape[1:], lambda r: (r, 0, 0))],
        out_specs=pl.BlockSpec((None, 16, wc.shape[2]), lambda r: (r, 0, 0)),
        out_shape=jax.ShapeDtypeStruct((2, 16, wc.shape[2]), F32),
    )(pe2, wc)


def _cmp_finish(part, cb, w2):
    nch = part.shape[0]
    bias = cb[0:1, 0:HEAD_DIM] + cb[1:2, HEAD_DIM:2 * HEAD_DIM]
    pre = bias + part[:, 0:HEAD_DIM] + pltpu.roll(part[:, HEAD_DIM:2 * HEAD_DIM], nch - 1, 0)
    return _dot(_gelu(pre).astype(BF16), w2)


def _cmp_prompt_kernel(kv0_ref, kv1_ref, wc_ref, w2_ref, cb_ref, o_ref, x_ref, *, nch):
    for g, kv_ref in enumerate((kv0_ref, kv1_ref)):
        for s in range(CMP_STRIDE):
            x = kv_ref[pl.ds(s, nch, stride=CMP_STRIDE), :]
            x_ref[g * nch:(g + 1) * nch, s * HEAD_DIM:(s + 1) * HEAD_DIM] = x.astype(BF16)
    part = _dot(x_ref[...], wc_ref[...])
    for g in range(G_B):
        o_ref[g] = _cmp_finish(part[g * nch:(g + 1) * nch], cb_ref[...], w2_ref[...])


def _cmp_prompt(kv, wc, w2, cb, B, T):
    nch = T // CMP_STRIDE
    assert nch % 16 == 0
    return pl.pallas_call(
        functools.partial(_cmp_prompt_kernel, nch=nch),
        grid=(B, 2),
        in_specs=[pl.BlockSpec((T, HEAD_DIM), lambda b, r: (b, G_B * r)),
                  pl.BlockSpec((T, HEAD_DIM), lambda b, r: (b, G_B * r + 1)),
                  pl.BlockSpec((None,) + wc.shape[1:], lambda b, r: (r, 0, 0)),
                  pl.BlockSpec((None, HEAD_DIM, HEAD_DIM), lambda b, r: (r, 0, 0)),
                  pl.BlockSpec((None, 16, wc.shape[2]), lambda b, r: (r, 0, 0))],
        out_specs=pl.BlockSpec((None, None, G_B, nch, HEAD_DIM), lambda b, r: (b, r, 0, 0, 0)),
        out_shape=jax.ShapeDtypeStruct((B, 2, G_B, nch, HEAD_DIM), F32),
        scratch_shapes=[pltpu.VMEM((G_B * nch, CMP_STRIDE * HEAD_DIM), BF16)],
        compiler_params=_cparams(("parallel", "parallel")),
    )(kv, kv, wc, w2, cb)


def _stack_heads(q, hpg):
    return jnp.concatenate([q[:, r * HEAD_DIM:(r + 1) * HEAD_DIM] for r in range(hpg)],
                           axis=0).astype(BF16)


def _softmax_pieces(pieces, hpg, tq, slopes):
    outs, probs = [], [[] for _ in pieces]
    for r in range(hpg):
        ss = []
        for (s, dist, mask, _) in pieces:
            sr = s[r * tq:(r + 1) * tq] - slopes[r] * dist
            ss.append(jnp.where(mask, sr, NEG))
        m = ss[0].max(axis=-1, keepdims=True)
        for sr in ss[1:]:
            m = jnp.maximum(m, sr.max(axis=-1, keepdims=True))
        es = [jnp.where(mask, jnp.exp(sr - m), 0.0) for sr, (_, _, mask, _) in zip(ss, pieces)]
        l = es[0].sum(axis=-1, keepdims=True)
        for ee in es[1:]:
            l = l + ee.sum(axis=-1, keepdims=True)
        inv = 1.0 / jnp.maximum(l, 1e-30)
        for i, ee in enumerate(es):
            probs[i].append(ee * inv)
    probs = [jnp.concatenate(p, axis=0) for p in probs]
    o = _dot(probs[0].astype(BF16), pieces[0][3])
    for p, piece in zip(probs[1:], pieces[1:]):
        o = o + _dot(p.astype(BF16), piece[3])
    return probs, o


def _sel_score_mask(imp, msel, qpos, nsel, ns):
    hi = imp.astype(BF16).astype(F32)
    mid = (imp - hi).astype(BF16).astype(F32)
    lo = imp - hi - mid
    p = _dot_fewrows(hi, msel) + _dot_fewrows(mid, msel) + _dot_fewrows(lo, msel)
    t, nsp = p.shape
    by_rows = t >= nsp and t % HEAD_DIM == 0
    if by_rows:
        nsr = -(-ns // SUBLANES) * SUBLANES
        p = p.T[:nsr]
        j = lax.broadcasted_iota(jnp.int32, (nsr, 1), 0)
        cur = (qpos[0:1] + lax.broadcasted_iota(jnp.int32, (1, t), 1)) // SEL_BLOCK
    else:
        j = lax.broadcasted_iota(jnp.int32, (1, nsp), 1)
        cur = qpos // SEL_BLOCK
    valid = (j <= cur) & (j < ns)
    forced = (j == 0) | (j == cur) | (j == cur - 1)
    score = jnp.where(valid, p + jnp.where(forced, FORCE_BONUS, 0.0), NEG)
    cnt = jnp.zeros(score.shape, F32)
    for i in range(ns):
        other = score[i:i + 1] if by_rows else score[:, i:i + 1]
        beats = (other > score) | ((other == score) & (i < j))
        cnt = cnt + jnp.where(beats, 1.0, 0.0)
    sel = jnp.where((cnt < nsel) & valid, 1.0, 0.0)
    if by_rows:
        sel = jnp.concatenate([sel, jnp.zeros((nsp - nsr, t), F32)], axis=0).T if nsr < nsp else sel.T
    return sel


def _nsa_prompt_kernel(sl_ref, q_ref, gt_ref, kc_ref, vc_ref, ks_ref, vs_ref, kw_ref, vw_ref,
                       msel_ref, paug_ref, srow_ref, o_ref, *, tq, tk, hpg, nsel, ns, T):
    g = pl.program_id(1)
    qi = pl.program_id(2)
    q0 = qi * tq
    slopes = [sl_ref[g * hpg + r] for r in range(hpg)]
    qpos = q0 + lax.broadcasted_iota(jnp.int32, (tq, 1), 0)
    lane = lax.broadcasted_iota(jnp.int32, (1, HEAD_DIM), 1)
    Q = _stack_heads(q_ref[...] * SCALE, hpg)

    ncp = kc_ref.shape[0]
    end = lax.broadcasted_iota(jnp.int32, (1, ncp), 1) * CMP_STRIDE + (CMP_BLOCK - 1)
    dist = qpos - end
    sc = _dot_nt(Q, kc_ref[...].astype(BF16))
    (pc,), o_cmp = _softmax_pieces([(sc, dist.astype(F32), dist >= 0, vc_ref[...].astype(BF16))],
                                   hpg, tq, slopes)
    imp = pc[0:tq]
    for r in range(1, hpg):
        imp = imp + pc[r * tq:(r + 1) * tq]
    selm = _sel_score_mask(imp, msel_ref[...], qpos, nsel, ns)

    srows = [srow_ref[pl.ds(g * hpg + r, 1), :] for r in range(hpg)]
    selneg = jnp.where((selm > 0.5) | (lane >= ns), 0.0, MASK_BF16)
    q_sel = jnp.concatenate(
        [Q, jnp.concatenate([selneg + sr for sr in srows], axis=0).astype(BF16)], axis=1)
    q_win = jnp.concatenate(
        [Q, jnp.concatenate([jnp.broadcast_to(sr, (tq, HEAD_DIM)) for sr in srows], axis=0).astype(BF16)],
        axis=1)
    ones_col = jnp.where(lane == 0, 1.0, 0.0).astype(BF16)

    def attend_tile(q_aug, k_ref, v_ref, k0, n, bias, ms, acc):
        k_aug = jnp.concatenate([k_ref[pl.ds(k0, n), :].astype(BF16), paug_ref[pl.ds(k0, n), :]], axis=1)
        v_aug = jnp.concatenate([v_ref[pl.ds(k0, n), :].astype(BF16),
                                 jnp.broadcast_to(ones_col, (n, HEAD_DIM))], axis=1)
        s = _dot_nt(q_aug, k_aug)
        new_ms, ps, alphas = [], [], []
        for r in range(hpg):
            sr = s[r * tq:(r + 1) * tq]
            if bias is not None:
                sr = sr + bias
            m_new = jnp.maximum(ms[r], sr.max(axis=-1, keepdims=True))
            new_ms.append(m_new)
            ps.append(jnp.exp(sr - m_new))
            alphas.append(jnp.broadcast_to(jnp.exp(ms[r] - m_new), (tq, 2 * HEAD_DIM)))
        pv = _dot(jnp.concatenate(ps, axis=0).astype(BF16), v_aug)
        return tuple(new_ms), jnp.concatenate(alphas, axis=0) * acc + pv

    def finish(acc):
        return acc[:, :HEAD_DIM] / jnp.maximum(acc[:, HEAD_DIM:HEAD_DIM + 1], 1e-30)

    init = (tuple(jnp.full((tq, 1), NEG, F32) for _ in range(hpg)),
            jnp.zeros((hpg * tq, 2 * HEAD_DIM), F32))

    nkt = (q0 + tq + tk - 1) // tk

    def kt_body(kt, carry):
        return attend_tile(q_sel, ks_ref, vs_ref, pl.multiple_of(kt * tk, tk), tk, None, *carry)

    ms, acc = lax.fori_loop(0, nkt - 1, kt_body, init)
    k0 = pl.multiple_of((nkt - 1) * tk, tk)
    causal = jnp.where(qpos >= k0 + lax.broadcasted_iota(jnp.int32, (1, tk), 1), 0.0, NEG)
    _, acc = attend_tile(q_sel, ks_ref, vs_ref, k0, tk, causal, ms, acc)
    o_sel = finish(acc)

    wl = WINDOW + tq
    kstart = pl.multiple_of(jnp.maximum(q0 - WINDOW, 0), tq)
    dw = qpos - (kstart + lax.broadcasted_iota(jnp.int32, (1, wl), 1))
    band = jnp.where((dw >= 0) & (dw < WINDOW), 0.0, NEG)
    _, acc = attend_tile(q_win, kw_ref, vw_ref, kstart, wl, band, *init)
    o_win = finish(acc)

    gt = _sigmoid(gt_ref[...])
    for r in range(hpg):
        rows = slice(r * tq, (r + 1) * tq)
        o_ref[:, r * HEAD_DIM:(r + 1) * HEAD_DIM] = (
            gt[:, 3 * r:3 * r + 1] * o_cmp[rows] + gt[:, 3 * r + 1:3 * r + 2] * o_sel[rows]
            + gt[:, 3 * r + 2:3 * r + 3] * o_win[rows])


def _alibi_slopes_np(h_b):
    h = np.arange(1, h_b + 1, dtype=np.float32)
    return np.exp2(-8.0 * h / h_b).astype(np.float32)


def _alibi_slopes(h_b):
    return jnp.asarray(_alibi_slopes_np(h_b))


def _msel_matrix(ncp, nsp, nc, ns):
    i = np.arange(ncp)[:, None]
    j = np.arange(nsp)[None, :]
    r = SEL_BLOCK // CMP_STRIDE
    back = CMP_BLOCK // CMP_STRIDE - 1
    m = (i >= r * j - back) & (i <= r * j + r - 1) & (i < nc) & (j < ns)
    return jnp.asarray(m.astype(np.float32), dtype=BF16)


def _expand_matrix(nsteps, nsp, tk):
    s = np.arange(nsteps)[:, None, None]
    j = np.arange(nsp)[None, :, None]
    key = np.arange(tk)[None, None, :]
    return jnp.asarray(((s * tk + key) // SEL_BLOCK == j).astype(np.float32), dtype=BF16)


def _nsa_prompt(u, kv, kcv, B, T, hpg):
    tq, tk = 256, 512
    assert T % tk == 0 and T >= WINDOW + tq
    ncp = T // CMP_STRIDE
    nc = ncp - CMP_BLOCK // CMP_STRIDE + 1
    ns = -(-T // SEL_BLOCK)
    nsp = HEAD_DIM
    assert ns <= POS_LANE and T <= 256 * HEAD_DIM
    nsel = min(N_SEL, ns)
    nqt = T // tq
    qw = hpg * HEAD_DIM
    gate_block0 = (G_B * qw + MEM_WIDTH) // HEAD_DIM
    msel = _msel_matrix(ncp, nsp, nc, ns)
    kpos = np.arange(T)
    pa = np.zeros((T, HEAD_DIM), np.float32)
    pa[kpos, kpos // SEL_BLOCK] = 1.0
    pa[:, POS_LANE:POS_LANE + 3] = (kpos // HEAD_DIM * HEAD_DIM)[:, None]
    pa[:, POS_LANE + 3:POS_LANE + 6] = (kpos % HEAD_DIM)[:, None]
    paug = jnp.asarray(pa, dtype=BF16)
    sl = np.asarray(_alibi_slopes_np(G_B * hpg))
    s1 = sl.astype(ml_dtypes.bfloat16).astype(np.float32)
    s2 = (sl - s1).astype(ml_dtypes.bfloat16).astype(np.float32)
    s3 = (sl - s1 - s2).astype(ml_dtypes.bfloat16).astype(np.float32)
    sr = np.zeros((-(-G_B * hpg // SUBLANES) * SUBLANES, HEAD_DIM), np.float32)
    for c, piece in enumerate((s1, s2, s3, s1, s2, s3)):
        sr[:G_B * hpg, POS_LANE + c] = piece
    srow = jnp.asarray(sr)

    def kvcol(c):
        return pl.BlockSpec((T, HEAD_DIM), lambda b, g, t: (b, c + g))

    def cmpspec(r):
        return pl.BlockSpec((None, None, None, ncp, HEAD_DIM), lambda b, g, t: (b, r, g, 0, 0))

    return pl.pallas_call(
        functools.partial(_nsa_prompt_kernel, tq=tq, tk=tk, hpg=hpg, nsel=nsel, ns=ns, T=T),
        grid=(B, G_B, nqt),
        in_specs=[pl.BlockSpec(memory_space=pltpu.SMEM),
                  pl.BlockSpec((tq, qw), lambda b, g, t: (b * nqt + t, g)),
                  pl.BlockSpec((tq, HEAD_DIM), lambda b, g, t: (b * nqt + t, gate_block0 + g)),
                  cmpspec(0), cmpspec(1), kvcol(4), kvcol(6), kvcol(8), kvcol(10),
                  pl.BlockSpec(msel.shape, lambda b, g, t: (0, 0)),
                  pl.BlockSpec(paug.shape, lambda b, g, t: (0, 0)),
                  pl.BlockSpec(srow.shape, lambda b, g, t: (0, 0))],
        out_specs=pl.BlockSpec((tq, qw), lambda b, g, t: (b * nqt + t, g)),
        out_shape=jax.ShapeDtypeStruct((B * T, G_B * qw), F32),
        compiler_params=_cparams(("parallel", "parallel", "arbitrary")),
    )(_alibi_slopes(G_B * hpg), u, u, kcv, kcv, kv, kv, kv, kv, msel, paug, srow)


CMP_PAGES = 16
SEL_PAGES = 8


def _cmp_pages_kernel(pt_ref, *refs, npage):
    page_refs = refs[:npage]
    wc_ref, o_ref, x_ref = refs[npage:]
    cpp = PAGE_SIZE // CMP_STRIDE
    rows = npage * cpp
    slabs = page_refs[0].shape[0] // PAGE_SIZE
    for rt in range(2):
        for g in range(G_B):
            for p in range(npage):
                for s in range(CMP_STRIDE):
                    x_ref[rt, g * rows + p * cpp:g * rows + (p + 1) * cpp, s * HEAD_DIM:(s + 1) * HEAD_DIM] = (
                        page_refs[p][pl.ds(s * slabs + rt * G_B + g, cpp, stride=CMP_STRIDE * slabs), :])
    for rt in range(2):
        part = _dot(x_ref[rt].astype(BF16), wc_ref[rt])
        for g in range(G_B):
            o_ref[rt, g] = part[g * rows:(g + 1) * rows]


def _cmp_pages(cache, page_table, wc):
    Bs, npg = page_table.shape
    npage = CMP_PAGES
    assert npg % npage == 0
    cpp = PAGE_SIZE // CMP_STRIDE
    rows = npage * cpp
    nch = npg * cpp

    def page_spec(p):
        return pl.BlockSpec((None,) + cache.shape[1:], lambda b, s, pt: (pt[b, s * npage + p], 0, 0))

    grid_spec = pltpu.PrefetchScalarGridSpec(
        num_scalar_prefetch=1,
        grid=(Bs, npg // npage),
        in_specs=[page_spec(p) for p in range(npage)]
        + [pl.BlockSpec(wc.shape, lambda b, s, pt: (0, 0, 0))],
        out_specs=pl.BlockSpec((None, 2, G_B, rows, wc.shape[2]), lambda b, s, pt: (b, 0, 0, s, 0)),
        scratch_shapes=[pltpu.VMEM((2, G_B * rows, CMP_STRIDE * HEAD_DIM), F32)],
    )
    return pl.pallas_call(
        functools.partial(_cmp_pages_kernel, npage=npage),
        grid_spec=grid_spec,
        out_shape=jax.ShapeDtypeStruct((Bs, 2, G_B, nch, wc.shape[2]), F32),
        compiler_params=_cparams(("parallel", "arbitrary")),
    )(page_table, *([cache] * npage), wc)


def _nsa_sample_cmp_kernel(sl_ref, part_ref, cb_ref, w2_ref, q_ref, msel_ref, ocmp_ref, selm_ref,
                           *, tp, hpg, nsel, ns, past):
    g = pl.program_id(1)
    slopes = [sl_ref[g * hpg + r] for r in range(hpg)]
    kc = _cmp_finish(part_ref[0], cb_ref[0], w2_ref[0]).astype(BF16)
    vc = _cmp_finish(part_ref[1], cb_ref[1], w2_ref[1]).astype(BF16)
    nch = kc.shape[0]
    qpos = past + lax.broadcasted_iota(jnp.int32, (tp, 1), 0)
    Q = _stack_heads(q_ref[...], hpg)
    idx = lax.broadcasted_iota(jnp.int32, (1, nch), 1)
    dist = qpos - (idx * CMP_STRIDE + (CMP_BLOCK - 1))
    mask = (dist >= 0) & (idx < nch - 1)
    sc = _dot_nt(Q, kc) * SCALE
    (pc,), o_cmp = _softmax_pieces([(sc, dist.astype(F32), mask, vc)], hpg, tp, slopes)
    imp = pc[0:tp]
    for r in range(1, hpg):
        imp = imp + pc[r * tp:(r + 1) * tp]
    ocmp_ref[...] = o_cmp
    selm_ref[...] = _sel_score_mask(imp, msel_ref[...], qpos, nsel, ns)


def _nsa_sample_cmp(u, part, cb, w2, Bs, tp, hpg, past):
    nch = part.shape[3]
    nc = nch - CMP_BLOCK // CMP_STRIDE + 1
    ns = past // SEL_BLOCK + 1
    nsp = -(-ns // 128) * 128
    nsel = min(N_SEL, ns)
    qw = hpg * HEAD_DIM
    msel = _msel_matrix(nch, nsp, nc, ns)
    return pl.pallas_call(
        functools.partial(_nsa_sample_cmp_kernel, tp=tp, hpg=hpg, nsel=nsel, ns=ns, past=past),
        grid=(Bs, G_B),
        in_specs=[pl.BlockSpec(memory_space=pltpu.SMEM),
                  pl.BlockSpec((None, 2, None, nch, part.shape[4]), lambda b, g: (b, 0, g, 0, 0)),
                  pl.BlockSpec(cb.shape, lambda b, g: (0, 0, 0)),
                  pl.BlockSpec(w2.shape, lambda b, g: (0, 0, 0)),
                  pl.BlockSpec((None, tp, qw), lambda b, g: (b, 0, g)),
                  pl.BlockSpec(msel.shape, lambda b, g: (0, 0))],
        out_specs=[pl.BlockSpec((None, None, hpg * tp, HEAD_DIM), lambda b, g: (b, g, 0, 0)),
                   pl.BlockSpec((None, None, tp, nsp), lambda b, g: (b, g, 0, 0))],
        out_shape=[jax.ShapeDtypeStruct((Bs, G_B, hpg * tp, HEAD_DIM), F32),
                   jax.ShapeDtypeStruct((Bs, G_B, tp, nsp), F32)],
        compiler_params=_cparams(("parallel", "parallel")),
    )(_alibi_slopes(G_B * hpg), part, cb, w2, u, msel)


def _nsa_sample_kernel(pt_ref, sl_ref, *refs, npage, tp, hpg, past):
    page_refs = refs[:npage]
    q_ref, gt_ref, selm_ref, ocmp_ref, kvn_ref, cwin_ref, e_ref, o_ref, m_ref, l_ref, acc_ref = refs[npage:]
    s_id = pl.program_id(1)
    nsteps = pl.num_programs(1)
    tk = npage * PAGE_SIZE
    rows = hpg * tp
    qpos = past + lax.broadcasted_iota(jnp.int32, (tp, 1), 0)

    @pl.when(s_id == 0)
    def _():
        m_ref[...] = jnp.full(m_ref.shape, NEG, F32)
        l_ref[...] = jnp.zeros(l_ref.shape, F32)
        acc_ref[...] = jnp.zeros(acc_ref.shape, F32)

    kpos = s_id * tk + lax.broadcasted_iota(jnp.int32, (1, tk), 1)
    d = qpos - kpos
    df = d.astype(F32)
    for g in range(G_B):
        slopes = [sl_ref[g * hpg + r] for r in range(hpg)]
        Q = _stack_heads(q_ref[:, g * hpg * HEAD_DIM:(g + 1) * hpg * HEAD_DIM], hpg)
        slabs = page_refs[0].shape[0] // PAGE_SIZE
        kb = jnp.concatenate([page_refs[p][pl.ds(2 * G_B + g, PAGE_SIZE, stride=slabs), :]
                              for p in range(npage)], axis=0).astype(BF16)
        vb = jnp.concatenate([page_refs[p][pl.ds(3 * G_B + g, PAGE_SIZE, stride=slabs), :]
                              for p in range(npage)], axis=0).astype(BF16)
        s = _dot_nt(Q, kb) * SCALE
        mask = (_dot_fewrows(selm_ref[g], e_ref[...]) > 0.5) & (d >= 0)
        ps, alphas = [], []
        for r in range(hpg):
            hr = slice(r * tp, (r + 1) * tp)
            sr = jnp.where(mask, s[hr] - slopes[r] * df, NEG)
            m_old = m_ref[g, hr]
            m_new = jnp.maximum(m_old, sr.max(axis=-1, keepdims=True))
            alpha = jnp.exp(m_old - m_new)
            p = jnp.where(mask, jnp.exp(sr - m_new), 0.0)
            m_ref[g, hr] = m_new
            l_ref[g, hr] = alpha * l_ref[g, hr] + p.sum(axis=-1, keepdims=True)
            ps.append(p)
            alphas.append(jnp.broadcast_to(alpha, (tp, HEAD_DIM)))
        pv = _dot(jnp.concatenate(ps, axis=0).astype(BF16), vb)
        acc_ref[g] = jnp.concatenate(alphas, axis=0) * acc_ref[g] + pv

    @pl.when(s_id == nsteps - 1)
    def _():
        gt = _sigmoid(gt_ref[...])
        npad = HEAD_DIM
        kidx = lax.broadcasted_iota(jnp.int32, (1, npad), 1)
        dn = qpos - (past + kidx)
        new_ok = (kidx < tp) & (dn >= 0)
        zpad = jnp.zeros((npad - tp, HEAD_DIM), F32)
        nblk = past // SEL_BLOCK
        wpast = cwin_ref.shape[0] // (2 * G_B)
        dwp = qpos - (past - wpast + lax.broadcasted_iota(jnp.int32, (1, wpast), 1))
        for g in range(G_B):
            slopes = [sl_ref[g * hpg + r] for r in range(hpg)]
            Q = _stack_heads(q_ref[:, g * hpg * HEAD_DIM:(g + 1) * hpg * HEAD_DIM], hpg)

            def newrows(c):
                blk = kvn_ref[:, (c * G_B + g) * HEAD_DIM:(c * G_B + g + 1) * HEAD_DIM]
                return jnp.concatenate([blk, zpad], axis=0).astype(BF16)

            kn, vn = newrows(2), newrows(3)
            sn = _dot_nt(Q, kn) * SCALE
            mask_n = new_ok & (selm_ref[g][:, nblk:nblk + 1] > 0.5)
            o_sel = []
            for r in range(hpg):
                hr = slice(r * tp, (r + 1) * tp)
                sr = jnp.where(mask_n, sn[hr] - slopes[r] * dn.astype(F32), NEG)
                m_old = m_ref[g, hr]
                m_new = jnp.maximum(m_old, sr.max(axis=-1, keepdims=True))
                alpha = jnp.exp(m_old - m_new)
                p = jnp.where(mask_n, jnp.exp(sr - m_new), 0.0)
                l = alpha * l_ref[g, hr] + p.sum(axis=-1, keepdims=True)
                acc = alpha * acc_ref[g, hr] + _dot_fewrows(p, vn)
                o_sel.append(acc / jnp.maximum(l, 1e-30))
            kwp = cwin_ref[pl.ds(g, wpast, stride=2 * G_B), :].astype(BF16)
            vwp = cwin_ref[pl.ds(G_B + g, wpast, stride=2 * G_B), :].astype(BF16)
            kwn, vwn = newrows(4), newrows(5)
            pieces = [(_dot_nt(Q, kwp) * SCALE, dwp.astype(F32), (dwp >= 0) & (dwp < WINDOW), vwp),
                      (_dot_nt(Q, kwn) * SCALE, dn.astype(F32), new_ok & (dn < WINDOW), vwn)]
            _, o_win = _softmax_pieces(pieces, hpg, tp, slopes)
            o_cmp = ocmp_ref[g]
            for r in range(hpg):
                hr = slice(r * tp, (r + 1) * tp)
                c = 3 * (g * hpg + r)
                col = (g * hpg + r) * HEAD_DIM
                o_ref[:, col:col + HEAD_DIM] = (gt[:, c:c + 1] * o_cmp[hr] + gt[:, c + 1:c + 2] * o_sel[r]
                                                + gt[:, c + 2:c + 3] * o_win[hr])


def _nsa_sample(u, cache, page_table, selm, o_cmp, kvn, cwin, Bs, tp, hpg, past):
    npg = page_table.shape[1]
    npage = SEL_PAGES
    assert npg % npage == 0
    nsteps = npg // npage
    tk = npage * PAGE_SIZE
    nsp = selm.shape[3]
    qw = G_B * hpg * HEAD_DIM
    e3 = _expand_matrix(nsteps, nsp, tk)
    gates = jnp.concatenate([u[:, :, qw + MEM_WIDTH + g * HEAD_DIM:qw + MEM_WIDTH + g * HEAD_DIM + 3 * hpg]
                             for g in range(G_B)]
                            + [jnp.zeros((Bs, tp, HEAD_DIM - 3 * hpg * G_B), F32)], axis=2)

    def page_spec(p):
        return pl.BlockSpec((None,) + cache.shape[1:], lambda b, s, pt: (pt[b, s * npage + p], 0, 0))

    grid_spec = pltpu.PrefetchScalarGridSpec(
        num_scalar_prefetch=1,
        grid=(Bs, nsteps),
        in_specs=[pl.BlockSpec(memory_space=pltpu.SMEM)]
        + [page_spec(p) for p in range(npage)]
        + [pl.BlockSpec((None, tp, qw), lambda b, s, pt: (b, 0, 0)),
           pl.BlockSpec((None, tp, HEAD_DIM), lambda b, s, pt: (b, 0, 0)),
           pl.BlockSpec((None, G_B, tp, nsp), lambda b, s, pt: (b, 0, 0, 0)),
           pl.BlockSpec((None, G_B, hpg * tp, HEAD_DIM), lambda b, s, pt: (b, 0, 0, 0)),
           pl.BlockSpec((None, tp, kvn.shape[2]), lambda b, s, pt: (b, 0, 0)),
           pl.BlockSpec((None,) + cwin.shape[1:], lambda b, s, pt: (b, 0, 0)),
           pl.BlockSpec((None, nsp, tk), lambda b, s, pt: (s, 0, 0))],
        out_specs=pl.BlockSpec((None, tp, qw), lambda b, s, pt: (b, 0, 0)),
        scratch_shapes=[pltpu.VMEM((G_B, hpg * tp, 1), F32), pltpu.VMEM((G_B, hpg * tp, 1), F32),
                        pltpu.VMEM((G_B, hpg * tp, HEAD_DIM), F32)],
    )
    return pl.pallas_call(
        functools.partial(_nsa_sample_kernel, npage=npage, tp=tp, hpg=hpg, past=past),
        grid_spec=grid_spec,
        out_shape=jax.ShapeDtypeStruct((Bs, tp, qw), F32),
        compiler_params=_cparams(("parallel", "arbitrary")),
    )(page_table, _alibi_slopes(G_B * hpg), *([cache] * npage), u, gates, selm, o_cmp, kvn, cwin, e3)


def _prep_weights(w_in_a, w_in_b, w_o, w_mem_kv, w_kv_b, w_cmp1, w_cmp2, w_ffn_in, w_ffn_out, cmp_pos, hpg):
    tokw = w_in_b.shape[2] - 3 * G_B * hpg - MEM_WIDTH
    D = w_in_b.shape[1]
    wb = w_in_b[0]
    gates = wb[:, tokw:tokw + 3 * G_B * hpg]
    gpad = jnp.zeros((D, HEAD_DIM - 3 * hpg), F32)
    wb2 = jnp.concatenate([wb[:, :tokw], wb[:, tokw + 3 * G_B * hpg:]]
                          + sum([[gates[:, g * 3 * hpg:(g + 1) * 3 * hpg], gpad] for g in range(G_B)], []),
                          axis=1)
    r = CMP_BLOCK // CMP_STRIDE
    flat = CMP_STRIDE * HEAD_DIM
    wc = w_cmp1.reshape(2, r, flat, HEAD_DIM).transpose(0, 2, 1, 3).reshape(2, flat, r * HEAD_DIM)
    pe2 = jnp.concatenate([cmp_pos.reshape(2, r, flat), jnp.zeros((2, 16 - r, flat), F32)], axis=1)
    return dict(w_in_a=w_in_a.astype(BF16), w_in_b=wb2.astype(BF16), w_o=w_o.astype(BF16),
                w_mem_kv=w_mem_kv.astype(BF16), w_kv_b=w_kv_b.astype(BF16), wc=wc.astype(BF16),
                w2=w_cmp2.astype(BF16), w_ffn_in=w_ffn_in.astype(BF16), w_ffn_out=w_ffn_out.astype(BF16),
                pe2=pe2)


def _layer_tail(h, mix, mo, l, prm, conv_buf, B, T):
    g = prm['norm_gains'][l]
    h, xn = _mmres(mix, mo, (prm['w_o'], l), h, g[1], g_next=g[2])
    y, new_buf = _ffn_in(xn, (prm['w_ffn_in'], l), prm['w_ffn_conv'][l], prm['b_ffn_conv'][l],
                         None if conv_buf is None else conv_buf[l], B, T)
    h = _mmres(y, None, (prm['w_ffn_out'], l), h, g[3])
    return h, new_buf


def _trunk(x, mem_kv, hgrn_s0, conv_buf, prm, B, T, t_valid, nsa_fn):
    D = x.shape[-1]
    H = prm['hgrn_norm'].shape[1]
    h = x.reshape(B * T, D)
    u = _mm(h, prm['norm_gains'][0, 0], (prm['w_in_a'], 0)).reshape(B, T, -1)
    mix, s_fin = _hgrn(u, prm['lb_logits'], prm['hgrn_norm'][0], None if hgrn_s0 is None else hgrn_s0[0],
                       B, T, H, t_valid)
    mo = _memattn(u, 4 * H * HEAD_DIM // MEM_WIDTH, mem_kv, 0, B, T)
    h, buf0 = _layer_tail(h, mix.reshape(B * T, -1), mo.reshape(B * T, -1), 0, prm, conv_buf, B, T)
    n_slab = prm['w_kv_b'].shape[1] // HEAD_DIM
    kv, kv_rows, kv_win = _mm(h, prm['kv_norm'], prm['w_kv_b'], slabs=(n_slab - 2 * G_B, 2 * G_B))
    u = _mm(h, prm['norm_gains'][1, 0], prm['w_in_b'])
    mix = nsa_fn(u, kv)
    mo = _memattn(u.reshape(B, T, -1), H * HEAD_DIM // MEM_WIDTH, mem_kv, 1, B, T)
    h, buf1 = _layer_tail(h, mix.reshape(B * T, -1), mo.reshape(B * T, -1), 1, prm, conv_buf, B, T)
    return h.reshape(B, T, D), s_fin, (buf0, buf1), kv_rows, kv_win


def kernel(x_prompt, x_sample, mem_prompt, state_hgrn, cache_conv, cache_mem, cache_kv, cache_win,
           page_table, norm_gains, w_in_a, lb_logits, hgrn_norm, w_in_b, w_o, w_mem_kv, kv_norm,
           w_kv_b, cmp_pos, w_cmp1, w_cmp2, w_ffn_in, w_ffn_conv, b_ffn_conv, w_ffn_out):
    Bp, Tp, D = x_prompt.shape
    Bs, Ts, _ = x_sample.shape
    depth = norm_gains.shape[0]
    assert depth == 2 and w_in_a.shape[0] == 1 and w_in_b.shape[0] == 1
    H = hgrn_norm.shape[1]
    hpg = H // G_B
    ml = mem_prompt.shape[1]
    n_rows = cache_kv.shape[2]

    prm = _prep_weights(w_in_a, w_in_b, w_o, w_mem_kv, w_kv_b, w_cmp1, w_cmp2, w_ffn_in, w_ffn_out,
                        cmp_pos, hpg)
    prm.update(norm_gains=norm_gains, lb_logits=lb_logits, hgrn_norm=hgrn_norm, kv_norm=kv_norm,
               w_ffn_conv=w_ffn_conv, b_ffn_conv=b_ffn_conv)
    cb = _cmp_bias(prm['pe2'], prm['wc'])

    memx = mem_prompt.reshape(Bp * ml, D)
    mem_kv_p = [_mm(memx, None, (prm['w_mem_kv'], l), norm=False).reshape(Bp, ml, 2 * MEM_WIDTH)
                for l in range(depth)]

    def nsa_p(u, kv):
        kcv = _cmp_prompt(kv, prm['wc'], prm['w2'], cb, Bp, Tp)
        return _nsa_prompt(u, kv, kcv, Bp, Tp, hpg)

    y_p, hgrn_p, conv_p, kvr_p, kvw_p = _trunk(x_prompt, mem_kv_p, None, None, prm, Bp, Tp, Tp, nsa_p)

    tp = -(-Ts // SUBLANES) * SUBLANES
    past = page_table.shape[1] * PAGE_SIZE
    assert Ts < CMP_STRIDE and tp <= SEL_BLOCK and past >= WINDOW
    xs = jnp.concatenate([x_sample, jnp.zeros((Bs, tp - Ts, D), F32)], axis=1)

    assert n_rows == 4
    cache5 = cache_kv.reshape(cache_kv.shape[0], PAGE_SIZE * n_rows * G_B, HEAD_DIM)

    def nsa_s(u, kv):
        u3 = u.reshape(Bs, tp, -1)
        part = _cmp_pages(cache5, page_table, prm['wc'])
        o_cmp, selm = _nsa_sample_cmp(u3, part, cb, prm['w2'], Bs, tp, hpg, past)
        return _nsa_sample(u3, cache5, page_table, selm, o_cmp, kv.reshape(Bs, tp, -1),
                           cache_win.reshape(Bs, -1, HEAD_DIM), Bs, tp, hpg, past)

    y_s, hgrn_s, conv_s, kvr_s, kvw_s = _trunk(xs, cache_mem.reshape(depth, Bs, -1, HEAD_DIM), state_hgrn, cache_conv,
                                       prm, Bs, tp, Ts, nsa_s)

    wl = min(WINDOW, Tp)
    return (
        y_p,
        y_s[:, :Ts],
        hgrn_p[None],
        hgrn_s[None],
        jnp.stack(conv_p),
        jnp.stack([c[:, Ts - (CONV_W - 1):Ts] for c in conv_s]),
        jnp.stack(mem_kv_p).reshape(depth, Bp, ml, 2, MEM_HEADS, HEAD_DIM),
        kvr_p.reshape(Bp, Tp // PAGE_SIZE, PAGE_SIZE, n_rows, G_B, HEAD_DIM),
        kvr_s.reshape(Bs, tp, n_rows, G_B, HEAD_DIM)[:, :Ts],
        kvw_p.reshape(Bp, Tp, 2, G_B, HEAD_DIM)[:, Tp - wl:],
        kvw_s.reshape(Bs, tp, 2, G_B, HEAD_DIM)[:, :Ts],
    )
```

```python
import functools
import math

import ml_dtypes
import numpy as np
import jax
import jax.numpy as jnp
from jax import lax
from jax.experimental import pallas as pl
from jax.experimental.pallas import tpu as pltpu

F32 = jnp.float32
BF16 = jnp.bfloat16

HEAD_DIM = 128
MEM_HEADS = 4
MEM_WIDTH = MEM_HEADS * HEAD_DIM
G_B = 2
HGRN_CHUNK = 32
CMP_BLOCK = 32
CMP_STRIDE = 16
SEL_BLOCK = 64
N_SEL = 16
WINDOW = 512
PAGE_SIZE = 128
FORCE_BONUS = 1e4
CONV_W = 3
EPS = 1e-6
NEG = -1e30
MASK_BF16 = -2.0 ** 100
POS_LANE = 120
SCALE = HEAD_DIM ** -0.5
SUBLANES = 8
FFN_PIECE = 256
HGRN_HEADS = 6
VMEM_LIMIT = 56 * 1024 * 1024


def _cparams(sem):
    return pltpu.CompilerParams(dimension_semantics=sem, vmem_limit_bytes=VMEM_LIMIT)


def _dot(a, b):
    return jnp.dot(a, b, preferred_element_type=F32)


def _dot_nt(a, b):
    return lax.dot_general(a, b, (((1,), (1,)), ((), ())), preferred_element_type=F32)


def _dot_tn(a, b):
    return lax.dot_general(a, b, (((0,), (0,)), ((), ())), preferred_element_type=F32)


BF16_ROWS = 16


def _pad_rows(x):
    n = x.shape[0]
    if n % BF16_ROWS == 0:
        return x
    return jnp.concatenate([x, jnp.zeros((BF16_ROWS - n % BF16_ROWS,) + x.shape[1:], x.dtype)], axis=0)


def _dot_fewrows(a, b):
    return _dot(_pad_rows(a).astype(BF16), b)[:a.shape[0]]


def _rms(x, g):
    return x * lax.rsqrt(jnp.mean(x * x, axis=-1, keepdims=True) + EPS) * g


def _sigmoid(x):
    return 0.5 * jnp.tanh(0.5 * x) + 0.5


def _gelu(x):
    return 0.5 * x * (1.0 + jnp.tanh(0.7978845608028654 * (x + 0.044715 * (x * x * x))))


def _layer_weight(w, block, index_map):
    if isinstance(w, tuple):
        w3, l = w
        return w3, pl.BlockSpec((None,) + block, lambda *a: (l,) + index_map(*a))
    return w, pl.BlockSpec(block, index_map)


def _wshape(w):
    return w[0].shape[1:] if isinstance(w, tuple) else w.shape


def _pick(n, cands):
    for c in cands:
        if n % c == 0:
            return c
    raise ValueError(f"no tile for {n} in {cands}")


def _mm_kernel(x_ref, g_ref, w_ref, o_ref, *rest, norm, rows, slabs):
    xn_ref = rest[-1]
    tm = x_ref.shape[0]
    j = pl.program_id(1)

    @pl.when(j == 0)
    def _():
        def body(c, carry):
            r = pl.multiple_of(c * rows, rows)
            x = x_ref[pl.ds(r, rows), :]
            if norm:
                x = _rms(x, g_ref[...])
            xn_ref[pl.ds(r, rows), :] = x.astype(BF16)
            return carry
        lax.fori_loop(0, tm // rows, body, 0)

    res = _dot(xn_ref[...], w_ref[...])
    o_ref[...] = res
    if slabs:
        per_step = res.shape[1] // HEAD_DIM
        first = 0
        for s_ref, nslab in zip(rest[:-1], slabs):
            steps = nslab // per_step

            @pl.when((j >= first) & (j < first + steps))
            def _(s_ref=s_ref, nslab=nslab, first=first):
                for c in range(per_step):
                    s_ref[pl.ds((j - first) * per_step + c, tm, stride=nslab), :] = (
                        res[:, c * HEAD_DIM:(c + 1) * HEAD_DIM])
            first += steps


def _mm(x, g, w, *, norm=True, slabs=()):
    M, K = x.shape
    N = _wshape(w)[1]
    tm = _pick(M, (1024, 512, 256, 128, 64, 32, 16))
    tn = _pick(N, (1664, 1152, 512, 256, 128)) if not slabs else _pick(N, (512, 256, 128))
    w, w_spec = _layer_weight(w, (K, tn), lambda i, j: (0, j))
    rows = min(tm, 128)
    if g is None:
        g = jnp.ones((K,), F32)
    out_specs = [pl.BlockSpec((tm, tn), lambda i, j: (i, j))]
    out_shape = [jax.ShapeDtypeStruct((M, N), F32)]
    for n in slabs:
        assert (n * HEAD_DIM) % tn == 0
        out_specs.append(pl.BlockSpec((tm * n, HEAD_DIM), lambda i, j: (i, 0)))
        out_shape.append(jax.ShapeDtypeStruct((M * n, HEAD_DIM), F32))
    assert sum(slabs) in (0, N // HEAD_DIM)
    out = pl.pallas_call(
        functools.partial(_mm_kernel, norm=norm, rows=rows, slabs=tuple(slabs)),
        grid=(M // tm, N // tn),
        in_specs=[pl.BlockSpec((tm, K), lambda i, j: (i, 0)),
                  pl.BlockSpec((1, K), lambda i, j: (0, 0)),
                  w_spec],
        out_specs=out_specs,
        out_shape=out_shape,
        scratch_shapes=[pltpu.VMEM((tm, K), BF16)],
        compiler_params=_cparams(("parallel", "arbitrary")),
    )(x, g.reshape(1, K), w)
    return out if slabs else out[0]


def _mmres_kernel(*refs, nk, two, nxt):
    refs = list(refs)
    a1_ref = refs.pop(0)
    a = a1_ref[...].astype(BF16)
    if two:
        a = jnp.concatenate([a, refs.pop(0)[...].astype(BF16)], axis=1)
    w_ref, h_ref, g_ref = refs[:3]
    gn_ref = refs[3] if nxt else None
    o_ref = refs[4] if nxt else refs[3]
    xn_ref = refs[5] if nxt else None
    k = pl.program_id(1)
    part = _dot(a, w_ref[...])

    def finish(acc):
        hn = h_ref[...] + _rms(acc, g_ref[...])
        o_ref[...] = hn
        if nxt:
            xn_ref[...] = _rms(hn, gn_ref[...]).astype(BF16)

    if nk == 1:
        finish(part)
    else:
        @pl.when(k == 0)
        def _():
            o_ref[...] = part

        @pl.when((k > 0) & (k < nk - 1))
        def _():
            o_ref[...] += part

        @pl.when(k == nk - 1)
        def _():
            finish(o_ref[...] + part)


def _mmres(a1, a2, w, h, g, g_next=None):
    M, N = h.shape
    K = _wshape(w)[0]
    two = a2 is not None
    nk = 1 if two else 2
    tk = K // nk
    assert K % nk == 0 and tk % 128 == 0
    tm = _pick(M, (512, 256, 128, 64, 32, 16))
    if two:
        assert a1.shape[1] + a2.shape[1] == K
        in_specs = [pl.BlockSpec((tm, a1.shape[1]), lambda i, k: (i, 0)),
                    pl.BlockSpec((tm, a2.shape[1]), lambda i, k: (i, 0))]
        args = [a1, a2]
    else:
        in_specs = [pl.BlockSpec((tm, tk), lambda i, k: (i, k))]
        args = [a1]
    w, w_spec = _layer_weight(w, (tk, N), lambda i, k: (k, 0))
    in_specs += [w_spec,
                 pl.BlockSpec((tm, N), lambda i, k: (i, 0)),
                 pl.BlockSpec((1, N), lambda i, k: (0, 0))]
    args += [w, h, g.reshape(1, N)]
    out_specs = [pl.BlockSpec((tm, N), lambda i, k: (i, 0))]
    out_shape = [jax.ShapeDtypeStruct((M, N), F32)]
    nxt = g_next is not None
    if nxt:
        in_specs.append(pl.BlockSpec((1, N), lambda i, k: (0, 0)))
        args.append(g_next.reshape(1, N))
        out_specs.append(pl.BlockSpec((tm, N), lambda i, k: (i, 0)))
        out_shape.append(jax.ShapeDtypeStruct((M, N), BF16))
    out = pl.pallas_call(
        functools.partial(_mmres_kernel, nk=nk, two=two, nxt=nxt),
        grid=(M // tm, nk),
        in_specs=in_specs,
        out_specs=out_specs,
        out_shape=out_shape,
        compiler_params=_cparams(("parallel", "arbitrary")),
    )(*args)
    return out if nxt else out[0]


def _ffn_in_kernel(xn_ref, w_ref, wc_ref, bc_ref, p1_ref, p2_ref,
                   y_ref, a_ref, carry_ref, *, T, rparts, cparts):
    tm = xn_ref.shape[0]
    i = pl.program_id(0)
    j = pl.program_id(1)
    wc = wc_ref[...]
    bc = bc_ref[...]
    if T >= tm:
        tiles_per_batch = T // tm

        @pl.when(i % tiles_per_batch == 0)
        def _():
            carry_ref[j] = p1_ref[...]

        prev = carry_ref[j]
        pr, pc = tm // rparts, FFN_PIECE
        assert cparts * pc * 2 == w_ref.shape[1]
        row = lax.broadcasted_iota(jnp.int32, (pr, 1), 0)
        for ch in range(cparts):
            cs = slice(ch * pc, (ch + 1) * pc)
            older, newer = prev[SUBLANES - 2:SUBLANES - 1, cs], prev[SUBLANES - 1:SUBLANES, cs]
            for rh in range(rparts):
                rs = slice(rh * pr, (rh + 1) * pr)
                xn = xn_ref[rs, :]
                ab = _dot(xn, w_ref[:, 2 * ch * pc:2 * (ch + 1) * pc])
                a, b = ab[:, :pc], ab[:, pc:]
                a1 = jnp.where(row == 0, newer, pltpu.roll(a, 1, 0))
                a2 = jnp.where(row == 0, older, jnp.where(row == 1, newer, pltpu.roll(a, 2, 0)))
                c = bc[:, cs] + a2 * wc[0:1, cs] + a1 * wc[1:2, cs] + a * wc[2:3, cs]
                y_ref[rs, cs] = (_gelu(c) * b).astype(BF16)
                older, newer = a[pr - 2:pr - 1], a[pr - 1:pr]
            carry_ref[j, :, cs] = a[pr - SUBLANES:pr]
            a_ref[:, cs] = a[pr - SUBLANES:pr]
    else:
        xn = xn_ref[...]
        ab = _dot(xn, w_ref[...])
        pc = FFN_PIECE
        a = jnp.concatenate([ab[:, 2 * c * pc:(2 * c + 1) * pc] for c in range(cparts)], axis=1)
        b = jnp.concatenate([ab[:, (2 * c + 1) * pc:(2 * c + 2) * pc] for c in range(cparts)], axis=1)
        tmod = lax.broadcasted_iota(jnp.int32, (tm, 1), 0) % T
        a1 = jnp.where(tmod == 0, p1_ref[...], pltpu.roll(a, 1, 0))
        a2 = jnp.where(tmod < 2, p2_ref[...], pltpu.roll(a, 2, 0))
        a_ref[...] = a
        c = bc + a2 * wc[0:1] + a1 * wc[1:2] + a * wc[2:3]
        y_ref[...] = (_gelu(c) * b).astype(BF16)


def _ffn_in(x, w, wconv, bconv, buf, B, T):
    M, K = x.shape
    F = _wshape(w)[1] // 2
    tn = 512
    nj = F // tn
    w, w_spec = _layer_weight(w, (K, 2 * tn), lambda i, j: (0, j))
    assert F % tn == 0
    if buf is None:
        buf = jnp.zeros((B, CONV_W - 1, F), F32)
    wc = jnp.concatenate([wconv, jnp.zeros((SUBLANES - CONV_W, F), F32)], axis=0)
    if T >= 512:
        tm = _pick(T, (2048, 1024, 512))
        tpb = T // tm
        p1 = jnp.concatenate([jnp.zeros((B, SUBLANES - 2, F), F32), buf], axis=1)
        p2 = p1
        p_specs = [pl.BlockSpec((None, SUBLANES, tn), lambda i, j: (i // tpb, 0, j))] * 2
        a_shape = jax.ShapeDtypeStruct((M // tm, SUBLANES, F), F32)
        a_spec = pl.BlockSpec((None, SUBLANES, tn), lambda i, j: (i, 0, j))
    else:
        tm = M
        assert T >= 2 and M % T == 0
        z = jnp.zeros((B, 1, F), F32)
        p1 = jnp.concatenate([buf[:, 1:2]] + [z] * (T - 1), axis=1).reshape(M, F)
        p2 = jnp.concatenate([buf[:, 0:1], buf[:, 1:2]] + [z] * (T - 2), axis=1).reshape(M, F)
        p_specs = [pl.BlockSpec((tm, tn), lambda i, j: (i, j))] * 2
        a_shape = jax.ShapeDtypeStruct((M, F), F32)
        a_spec = pl.BlockSpec((tm, tn), lambda i, j: (i, j))
    y, a = pl.pallas_call(
        functools.partial(_ffn_in_kernel, T=T, rparts=max(tm // 1024, 1), cparts=tn // FFN_PIECE),
        grid=(M // tm, nj),
        in_specs=[pl.BlockSpec((tm, K), lambda i, j: (i, 0)),
                  w_spec,
                  pl.BlockSpec((SUBLANES, tn), lambda i, j: (0, j)),
                  pl.BlockSpec((1, tn), lambda i, j: (0, j))] + p_specs,
        out_specs=[pl.BlockSpec((tm, tn), lambda i, j: (i, j)), a_spec],
        out_shape=[jax.ShapeDtypeStruct((M, F), BF16), a_shape],
        scratch_shapes=[pltpu.VMEM((nj, SUBLANES, tn), F32)],
        compiler_params=_cparams(("arbitrary", "arbitrary")),
    )(x, w, wc, bconv.reshape(1, F), p1, p2)
    if T >= 512:
        new_buf = a.reshape(B, T // tm, SUBLANES, F)[:, -1, SUBLANES - (CONV_W - 1):]
    else:
        new_buf = a.reshape(B, T, F)
    return y, new_buf


def _cumsum_tile(x):
    row = lax.broadcasted_iota(jnp.int32, (SUBLANES, 1), 0)
    d = 1
    while d < SUBLANES:
        x = x + jnp.where(row >= d, pltpu.roll(x, d, 0), 0.0)
        d *= 2
    return x


def _hgrn_chunk(uq, uf, v, ug, lb, gn, st, C, CP):
    nsub = CP // SUBLANES
    row = lax.broadcasted_iota(jnp.int32, (CP, 1), 0)
    row8 = row[0:SUBLANES]
    q = uq * _sigmoid(uq)
    fg = lb + (1.0 - lb) * _sigmoid(uf)
    kk = 1.0 - fg
    lf = jnp.log(fg)
    if C < CP:
        kk = jnp.where(row < C, kk, 0.0)
        lf = jnp.where(row < C, lf, 0.0)
    b_tiles, ends, off = [], [], None
    for i in range(nsub):
        bi = _cumsum_tile(lf[i * SUBLANES:(i + 1) * SUBLANES])
        if off is not None:
            bi = bi + off
        off = bi[SUBLANES - 1:SUBLANES]
        b_tiles.append(bi)
        ends.append(off)
    b = jnp.concatenate(b_tiles, axis=0) if nsub > 1 else b_tiles[0]
    b_last = off
    o = _dot_nt(_pad_rows(q * jnp.exp(b)).astype(BF16), st.astype(BF16))[:CP]
    if nsub > 1:
        e_all = jnp.concatenate([jnp.broadcast_to(e, (SUBLANES, HEAD_DIM)) for e in ends], axis=0)
        ke = kk * jnp.exp(e_all - b)
        lhs, rhs = [], []
        for j in range(nsub - 1):
            lo, hi = j * SUBLANES, (j + 1) * SUBLANES
            lhs.append(jnp.where(row >= hi, q * jnp.exp(jnp.minimum(b - ends[j], 0.0)), 0.0))
            rhs.append(jnp.where((row >= lo) & (row < hi), ke, 0.0))
        a_off = _dot_nt(jnp.concatenate(lhs, axis=1).astype(BF16), jnp.concatenate(rhs, axis=1).astype(BF16))
        o = o + _dot(a_off.astype(BF16), v.astype(BF16))
        kd = ke * jnp.exp(b_last - e_all)
    else:
        kd = kk * jnp.exp(b_last - b)
    o_tiles = []
    for i in range(nsub):
        sl = slice(i * SUBLANES, (i + 1) * SUBLANES)
        oi, qi, bi = o[sl], q[sl], b_tiles[i]
        for s in range(i * SUBLANES, min((i + 1) * SUBLANES, C)):
            d = jnp.exp(jnp.minimum(bi - b[s:s + 1], 0.0))
            a = jnp.sum(qi * kk[s:s + 1] * d, axis=-1, keepdims=True)
            oi = oi + jnp.where(row8 >= s - i * SUBLANES, a, 0.0) * v[s:s + 1]
        o_tiles.append(oi)
    o = jnp.concatenate(o_tiles, axis=0) if nsub > 1 else o_tiles[0]
    st_new = st * jnp.exp(b_last) + _dot_tn(_pad_rows(v).astype(BF16), _pad_rows(kd).astype(BF16))
    return _rms(o, gn) * _sigmoid(ug), st_new


def _hgrn_kernel(uq_ref, uf_ref, uv_ref, ug_ref, lbl_ref, gn_ref, s0_ref, o_ref, sfin_ref,
                 st_ref, *, C, CP, nchunk, hb):
    t = pl.program_id(2)

    @pl.when(t == 0)
    def _():
        for h in range(hb):
            st_ref[h] = s0_ref[h].T

    lbl = lbl_ref[...]
    e = jnp.exp(lbl - jnp.max(lbl, axis=0, keepdims=True))
    lb = e[0:1] / jnp.sum(e, axis=0, keepdims=True)
    gn = gn_ref[...]

    def chunk(c, carry):
        r0 = pl.multiple_of(c * CP, CP)
        for h in range(hb):
            cs = slice(h * HEAD_DIM, (h + 1) * HEAD_DIM)
            out, st_new = _hgrn_chunk(uq_ref[pl.ds(r0, CP), cs], uf_ref[pl.ds(r0, CP), cs],
                                      uv_ref[pl.ds(r0, CP), cs], ug_ref[pl.ds(r0, CP), cs],
                                      lb[:, cs], gn[:, cs], st_ref[h], C, CP)
            st_ref[h] = st_new
            o_ref[pl.ds(r0, CP), cs] = out
        return carry

    lax.fori_loop(0, nchunk, chunk, 0)

    @pl.when(t == pl.num_programs(2) - 1)
    def _():
        for h in range(hb):
            sfin_ref[h] = st_ref[h].T


def _hgrn(u, lb_logits, gnorm, s0, B, T, H, t_valid):
    C = math.gcd(t_valid, HGRN_CHUNK)
    assert C == t_valid or T == t_valid
    CP = max(C, SUBLANES)
    tt = _pick(T, (512, 256, 128, 64, 32, 16, 8))
    nchunk = tt // CP
    if s0 is None:
        s0 = jnp.zeros((B, H, HEAD_DIM, HEAD_DIM), F32)
    W = H * HEAD_DIM
    hb = _pick(H, (HGRN_HEADS, 2, 1))
    hg = H // hb
    wb = hb * HEAD_DIM

    def col(part):
        return pl.BlockSpec((None, tt, wb), lambda b, h, t: (b, t, part * hg + h))

    return pl.pallas_call(
        functools.partial(_hgrn_kernel, C=C, CP=CP, nchunk=nchunk, hb=hb),
        grid=(B, hg, T // tt),
        in_specs=[col(0), col(1), col(2), col(3),
                  pl.BlockSpec((lb_logits.shape[0], wb), lambda b, h, t: (0, h)),
                  pl.BlockSpec((1, wb), lambda b, h, t: (0, h)),
                  pl.BlockSpec((None, hb, HEAD_DIM, HEAD_DIM), lambda b, h, t: (b, h, 0, 0))],
        out_specs=[pl.BlockSpec((None, tt, wb), lambda b, h, t: (b, t, h)),
                   pl.BlockSpec((None, hb, HEAD_DIM, HEAD_DIM), lambda b, h, t: (b, h, 0, 0))],
        out_shape=[jax.ShapeDtypeStruct((B, T, W), F32),
                   jax.ShapeDtypeStruct((B, H, HEAD_DIM, HEAD_DIM), F32)],
        scratch_shapes=[pltpu.VMEM((hb, HEAD_DIM, HEAD_DIM), F32)],
        compiler_params=_cparams(("parallel", "parallel", "arbitrary")),
    )(u, u, u, u, lb_logits, gnorm.reshape(1, W), s0)


def _memattn_kernel(q_ref, kv_ref, o_ref):
    for h in range(MEM_HEADS):
        sl = slice(h * HEAD_DIM, (h + 1) * HEAD_DIM)
        q = _pad_rows(q_ref[:, sl]).astype(BF16)
        if kv_ref.shape[1] == HEAD_DIM:
            ml = kv_ref.shape[0] // (2 * MEM_HEADS)
            k = kv_ref[pl.ds(h, ml, stride=2 * MEM_HEADS), :].astype(BF16)
            v = kv_ref[pl.ds(MEM_HEADS + h, ml, stride=2 * MEM_HEADS), :].astype(BF16)
        else:
            k = kv_ref[:, sl].astype(BF16)
            v = kv_ref[:, MEM_WIDTH + h * HEAD_DIM:MEM_WIDTH + (h + 1) * HEAD_DIM].astype(BF16)
        s = _dot_nt(q, k) * SCALE
        p = jnp.exp(s - jnp.max(s, axis=-1, keepdims=True))
        l = jnp.sum(p, axis=-1, keepdims=True)
        o_ref[:, sl] = (_dot(p.astype(BF16), v) / l)[:q_ref.shape[0]]


def _memattn(u, qblock, mkv, layer, B, T):
    tq = _pick(T, (512, 256, 128, 64, 32, 16, 8))
    if isinstance(mkv, (list, tuple)):
        mkv = mkv[layer]
        kv_spec = pl.BlockSpec((None,) + mkv.shape[1:], lambda b, t: (b, 0, 0))
    else:
        kv_spec = pl.BlockSpec((None, None) + mkv.shape[2:], lambda b, t: (layer, b, 0, 0))
    return pl.pallas_call(
        _memattn_kernel,
        grid=(B, T // tq),
        in_specs=[pl.BlockSpec((None, tq, MEM_WIDTH), lambda b, t: (b, t, qblock)), kv_spec],
        out_specs=pl.BlockSpec((None, tq, MEM_WIDTH), lambda b, t: (b, t, 0)),
        out_shape=jax.ShapeDtypeStruct((B, T, MEM_WIDTH), F32),
        compiler_params=_cparams(("parallel", "parallel")),
    )(u, mkv)


def _cmp_bias_kernel(pe_ref, wc_ref, o_ref):
    o_ref[...] = _dot(pe_ref[...].astype(BF16), wc_ref[...])


def _cmp_bias(pe2, wc):
    return pl.pallas_call(
        _cmp_bias_kernel,
        grid=(2,),
        in_specs=[pl.BlockSpec((None, 16, pe2.shape[2]), lambda r: (r, 0, 0)),
                  pl.BlockSpec((None,) + wc.shape[1:], lambda r: (r, 0, 0))],
        out_specs=pl.BlockSpec((None, 16, wc.shape[2]), lambda r: (r, 0, 0)),
        out_shape=jax.ShapeDtypeStruct((2, 16, wc.shape[2]), F32),
    )(pe2, wc)


def _cmp_finish(part, cb, w2):
    nch = part.shape[0]
    bias = cb[0:1, 0:HEAD_DIM] + cb[1:2, HEAD_DIM:2 * HEAD_DIM]
    pre = bias + part[:, 0:HEAD_DIM] + pltpu.roll(part[:, HEAD_DIM:2 * HEAD_DIM], nch - 1, 0)
    return _dot(_gelu(pre).astype(BF16), w2)


def _cmp_prompt_kernel(kv0_ref, kv1_ref, wc_ref, w2_ref, cb_ref, o_ref, x_ref, *, nch):
    for g, kv_ref in enumerate((kv0_ref, kv1_ref)):
        for s in range(CMP_STRIDE):
            x = kv_ref[pl.ds(s, nch, stride=CMP_STRIDE), :]
            x_ref[g * nch:(g + 1) * nch, s * HEAD_DIM:(s + 1) * HEAD_DIM] = x.astype(BF16)
    part = _dot(x_ref[...], wc_ref[...])
    for g in range(G_B):
        o_ref[g] = _cmp_finish(part[g * nch:(g + 1) * nch], cb_ref[...], w2_ref[...])


def _cmp_prompt(kv, wc, w2, cb, B, T):
    nch = T // CMP_STRIDE
    assert nch % 16 == 0
    return pl.pallas_call(
        functools.partial(_cmp_prompt_kernel, nch=nch),
        grid=(B, 2),
        in_specs=[pl.BlockSpec((T, HEAD_DIM), lambda b, r: (b, G_B * r)),
                  pl.BlockSpec((T, HEAD_DIM), lambda b, r: (b, G_B * r + 1)),
                  pl.BlockSpec((None,) + wc.shape[1:], lambda b, r: (r, 0, 0)),
                  pl.BlockSpec((None, HEAD_DIM, HEAD_DIM), lambda b, r: (r, 0, 0)),
                  pl.BlockSpec((None, 16, wc.shape[2]), lambda b, r: (r, 0, 0))],
        out_specs=pl.BlockSpec((None, None, G_B, nch, HEAD_DIM), lambda b, r: (b, r, 0, 0, 0)),
        out_shape=jax.ShapeDtypeStruct((B, 2, G_B, nch, HEAD_DIM), F32),
        scratch_shapes=[pltpu.VMEM((G_B * nch, CMP_STRIDE * HEAD_DIM), BF16)],
        compiler_params=_cparams(("parallel", "parallel")),
    )(kv, kv, wc, w2, cb)


def _stack_heads(q, hpg):
    return jnp.concatenate([q[:, r * HEAD_DIM:(r + 1) * HEAD_DIM] for r in range(hpg)],
                           axis=0).astype(BF16)


def _softmax_pieces(pieces, hpg, tq, slopes):
    outs, probs = [], [[] for _ in pieces]
    for r in range(hpg):
        ss = []
        for (s, dist, mask, _) in pieces:
            sr = s[r * tq:(r + 1) * tq] - slopes[r] * dist
            ss.append(jnp.where(mask, sr, NEG))
        m = ss[0].max(axis=-1, keepdims=True)
        for sr in ss[1:]:
            m = jnp.maximum(m, sr.max(axis=-1, keepdims=True))
        es = [jnp.where(mask, jnp.exp(sr - m), 0.0) for sr, (_, _, mask, _) in zip(ss, pieces)]
        l = es[0].sum(axis=-1, keepdims=True)
        for ee in es[1:]:
            l = l + ee.sum(axis=-1, keepdims=True)
        inv = 1.0 / jnp.maximum(l, 1e-30)
        for i, ee in enumerate(es):
            probs[i].append(ee * inv)
    probs = [jnp.concatenate(p, axis=0) for p in probs]
    o = _dot(probs[0].astype(BF16), pieces[0][3])
    for p, piece in zip(probs[1:], pieces[1:]):
        o = o + _dot(p.astype(BF16), piece[3])
    return probs, o


def _sel_score_mask(imp, msel, qpos, nsel, ns):
    hi = imp.astype(BF16).astype(F32)
    mid = (imp - hi).astype(BF16).astype(F32)
    lo = imp - hi - mid
    p = _dot_fewrows(hi, msel) + _dot_fewrows(mid, msel) + _dot_fewrows(lo, msel)
    t, nsp = p.shape
    by_rows = t >= nsp and t % HEAD_DIM == 0
    if by_rows:
        nsr = -(-ns // SUBLANES) * SUBLANES
        p = p.T[:nsr]
        j = lax.broadcasted_iota(jnp.int32, (nsr, 1), 0)
        cur = (qpos[0:1] + lax.broadcasted_iota(jnp.int32, (1, t), 1)) // SEL_BLOCK
    else:
        j = lax.broadcasted_iota(jnp.int32, (1, nsp), 1)
        cur = qpos // SEL_BLOCK
    valid = (j <= cur) & (j < ns)
    forced = (j == 0) | (j == cur) | (j == cur - 1)
    score = jnp.where(valid, p + jnp.where(forced, FORCE_BONUS, 0.0), NEG)
    cnt = jnp.zeros(score.shape, F32)
    for i in range(ns):
        other = score[i:i + 1] if by_rows else score[:, i:i + 1]
        beats = (other > score) | ((other == score) & (i < j))
        cnt = cnt + jnp.where(beats, 1.0, 0.0)
    sel = jnp.where((cnt < nsel) & valid, 1.0, 0.0)
    if by_rows:
        sel = jnp.concatenate([sel, jnp.zeros((nsp - nsr, t), F32)], axis=0).T if nsr < nsp else sel.T
    return sel


def _nsa_prompt_kernel(sl_ref, q_ref, gt_ref, kc_ref, vc_ref, ks_ref, vs_ref, kw_ref, vw_ref,
                       msel_ref, paug_ref, srow_ref, o_ref, *, tq, tk, hpg, nsel, ns, T):
    g = pl.program_id(1)
    qi = pl.program_id(2)
    q0 = qi * tq
    slopes = [sl_ref[g * hpg + r] for r in range(hpg)]
    qpos = q0 + lax.broadcasted_iota(jnp.int32, (tq, 1), 0)
    lane = lax.broadcasted_iota(jnp.int32, (1, HEAD_DIM), 1)
    Q = _stack_heads(q_ref[...] * SCALE, hpg)

    ncp = kc_ref.shape[0]
    end = lax.broadcasted_iota(jnp.int32, (1, ncp), 1) * CMP_STRIDE + (CMP_BLOCK - 1)
    dist = qpos - end
    sc = _dot_nt(Q, kc_ref[...].astype(BF16))
    (pc,), o_cmp = _softmax_pieces([(sc, dist.astype(F32), dist >= 0, vc_ref[...].astype(BF16))],
                                   hpg, tq, slopes)
    imp = pc[0:tq]
    for r in range(1, hpg):
        imp = imp + pc[r * tq:(r + 1) * tq]
    selm = _sel_score_mask(imp, msel_ref[...], qpos, nsel, ns)

    srows = [srow_ref[pl.ds(g * hpg + r, 1), :] for r in range(hpg)]
    selneg = jnp.where((selm > 0.5) | (lane >= ns), 0.0, MASK_BF16)
    q_sel = jnp.concatenate(
        [Q, jnp.concatenate([selneg + sr for sr in srows], axis=0).astype(BF16)], axis=1)
    q_win = jnp.concatenate(
        [Q, jnp.concatenate([jnp.broadcast_to(sr, (tq, HEAD_DIM)) for sr in srows], axis=0).astype(BF16)],
        axis=1)
    ones_col = jnp.where(lane == 0, 1.0, 0.0).astype(BF16)

    def attend_tile(q_aug, k_ref, v_ref, k0, n, bias, ms, acc):
        k_aug = jnp.concatenate([k_ref[pl.ds(k0, n), :].astype(BF16), paug_ref[pl.ds(k0, n), :]], axis=1)
        v_aug = jnp.concatenate([v_ref[pl.ds(k0, n), :].astype(BF16),
                                 jnp.broadcast_to(ones_col, (n, HEAD_DIM))], axis=1)
        s = _dot_nt(q_aug, k_aug)
        new_ms, ps, alphas = [], [], []
        for r in range(hpg):
            sr = s[r * tq:(r + 1) * tq]
            if bias is not None:
                sr = sr + bias
            m_new = jnp.maximum(ms[r], sr.max(axis=-1, keepdims=True))
            new_ms.append(m_new)
            ps.append(jnp.exp(sr - m_new))
            alphas.append(jnp.broadcast_to(jnp.exp(ms[r] - m_new), (tq, 2 * HEAD_DIM)))
        pv = _dot(jnp.concatenate(ps, axis=0).astype(BF16), v_aug)
        return tuple(new_ms), jnp.concatenate(alphas, axis=0) * acc + pv

    def finish(acc):
        return acc[:, :HEAD_DIM] / jnp.maximum(acc[:, HEAD_DIM:HEAD_DIM + 1], 1e-30)

    init = (tuple(jnp.full((tq, 1), NEG, F32) for _ in range(hpg)),
            jnp.zeros((hpg * tq, 2 * HEAD_DIM), F32))

    nkt = (q0 + tq + tk - 1) // tk

    def kt_body(kt, carry):
        return attend_tile(q_sel, ks_ref, vs_ref, pl.multiple_of(kt * tk, tk), tk, None, *carry)

    ms, acc = lax.fori_loop(0, nkt - 1, kt_body, init)
    k0 = pl.multiple_of((nkt - 1) * tk, tk)
    causal = jnp.where(qpos >= k0 + lax.broadcasted_iota(jnp.int32, (1, tk), 1), 0.0, NEG)
    _, acc = attend_tile(q_sel, ks_ref, vs_ref, k0, tk, causal, ms, acc)
    o_sel = finish(acc)

    wl = WINDOW + tq
    kstart = pl.multiple_of(jnp.maximum(q0 - WINDOW, 0), tq)
    dw = qpos - (kstart + lax.broadcasted_iota(jnp.int32, (1, wl), 1))
    band = jnp.where((dw >= 0) & (dw < WINDOW), 0.0, NEG)
    _, acc = attend_tile(q_win, kw_ref, vw_ref, kstart, wl, band, *init)
    o_win = finish(acc)

    gt = _sigmoid(gt_ref[...])
    for r in range(hpg):
        rows = slice(r * tq, (r + 1) * tq)
        o_ref[:, r * HEAD_DIM:(r + 1) * HEAD_DIM] = (
            gt[:, 3 * r:3 * r + 1] * o_cmp[rows] + gt[:, 3 * r + 1:3 * r + 2] * o_sel[rows]
            + gt[:, 3 * r + 2:3 * r + 3] * o_win[rows])


def _alibi_slopes_np(h_b):
    h = np.arange(1, h_b + 1, dtype=np.float32)
    return np.exp2(-8.0 * h / h_b).astype(np.float32)


def _alibi_slopes(h_b):
    return jnp.asarray(_alibi_slopes_np(h_b))


def _msel_matrix(ncp, nsp, nc, ns):
    i = np.arange(ncp)[:, None]
    j = np.arange(nsp)[None, :]
    r = SEL_BLOCK // CMP_STRIDE
    back = CMP_BLOCK // CMP_STRIDE - 1
    m = (i >= r * j - back) & (i <= r * j + r - 1) & (i < nc) & (j < ns)
    return jnp.asarray(m.astype(np.float32), dtype=BF16)


def _expand_matrix(nsteps, nsp, tk):
    s = np.arange(nsteps)[:, None, None]
    j = np.arange(nsp)[None, :, None]
    key = np.arange(tk)[None, None, :]
    return jnp.asarray(((s * tk + key) // SEL_BLOCK == j).astype(np.float32), dtype=BF16)


def _nsa_prompt(u, kv, kcv, B, T, hpg):
    tq, tk = 256, 512
    assert T % tk == 0 and T >= WINDOW + tq
    ncp = T // CMP_STRIDE
    nc = ncp - CMP_BLOCK // CMP_STRIDE + 1
    ns = -(-T // SEL_BLOCK)
    nsp = HEAD_DIM
    assert ns <= POS_LANE and T <= 256 * HEAD_DIM
    nsel = min(N_SEL, ns)
    nqt = T // tq
    qw = hpg * HEAD_DIM
    gate_block0 = (G_B * qw + MEM_WIDTH) // HEAD_DIM
    msel = _msel_matrix(ncp, nsp, nc, ns)
    kpos = np.arange(T)
    pa = np.zeros((T, HEAD_DIM), np.float32)
    pa[kpos, kpos // SEL_BLOCK] = 1.0
    pa[:, POS_LANE:POS_LANE + 3] = (kpos // HEAD_DIM * HEAD_DIM)[:, None]
    pa[:, POS_LANE + 3:POS_LANE + 6] = (kpos % HEAD_DIM)[:, None]
    paug = jnp.asarray(pa, dtype=BF16)
    sl = np.asarray(_alibi_slopes_np(G_B * hpg))
    s1 = sl.astype(ml_dtypes.bfloat16).astype(np.float32)
    s2 = (sl - s1).astype(ml_dtypes.bfloat16).astype(np.float32)
    s3 = (sl - s1 - s2).astype(ml_dtypes.bfloat16).astype(np.float32)
    sr = np.zeros((-(-G_B * hpg // SUBLANES) * SUBLANES, HEAD_DIM), np.float32)
    for c, piece in enumerate((s1, s2, s3, s1, s2, s3)):
        sr[:G_B * hpg, POS_LANE + c] = piece
    srow = jnp.asarray(sr)

    def kvcol(c):
        return pl.BlockSpec((T, HEAD_DIM), lambda b, g, t: (b, c + g))

    def cmpspec(r):
        return pl.BlockSpec((None, None, None, ncp, HEAD_DIM), lambda b, g, t: (b, r, g, 0, 0))

    return pl.pallas_call(
        functools.partial(_nsa_prompt_kernel, tq=tq, tk=tk, hpg=hpg, nsel=nsel, ns=ns, T=T),
        grid=(B, G_B, nqt),
        in_specs=[pl.BlockSpec(memory_space=pltpu.SMEM),
                  pl.BlockSpec((tq, qw), lambda b, g, t: (b * nqt + t, g)),
                  pl.BlockSpec((tq, HEAD_DIM), lambda b, g, t: (b * nqt + t, gate_block0 + g)),
                  cmpspec(0), cmpspec(1), kvcol(4), kvcol(6), kvcol(8), kvcol(10),
                  pl.BlockSpec(msel.shape, lambda b, g, t: (0, 0)),
                  pl.BlockSpec(paug.shape, lambda b, g, t: (0, 0)),
                  pl.BlockSpec(srow.shape, lambda b, g, t: (0, 0))],
        out_specs=pl.BlockSpec((tq, qw), lambda b, g, t: (b * nqt + t, g)),
        out_shape=jax.ShapeDtypeStruct((B * T, G_B * qw), F32),
        compiler_params=_cparams(("parallel", "parallel", "arbitrary")),
    )(_alibi_slopes(G_B * hpg), u, u, kcv, kcv, kv, kv, kv, kv, msel, paug, srow)


CMP_PAGES = 32
SEL_PAGES = 16


def _cmp_pages_kernel(pt_ref, *refs, npage):
    page_refs = refs[:npage]
    wc_ref, o_ref, x_ref = refs[npage:]
    cpp = PAGE_SIZE // CMP_STRIDE
    rows = npage * cpp
    slabs = page_refs[0].shape[0] // PAGE_SIZE
    for rt in range(2):
        for g in range(G_B):
            for p in range(npage):
                for s in range(CMP_STRIDE):
                    x_ref[rt, g * rows + p * cpp:g * rows + (p + 1) * cpp, s * HEAD_DIM:(s + 1) * HEAD_DIM] = (
                        page_refs[p][pl.ds(s * slabs + rt * G_B + g, cpp, stride=CMP_STRIDE * slabs), :])
    for rt in range(2):
        part = _dot(x_ref[rt].astype(BF16), wc_ref[rt])
        for g in range(G_B):
            o_ref[rt, g] = part[g * rows:(g + 1) * rows]


def _cmp_pages(cache, page_table, wc):
    Bs, npg = page_table.shape
    npage = CMP_PAGES
    assert npg % npage == 0
    cpp = PAGE_SIZE // CMP_STRIDE
    rows = npage * cpp
    nch = npg * cpp

    def page_spec(p):
        return pl.BlockSpec((None,) + cache.shape[1:], lambda b, s, pt: (pt[b, s * npage + p], 0, 0))

    grid_spec = pltpu.PrefetchScalarGridSpec(
        num_scalar_prefetch=1,
        grid=(Bs, npg // npage),
        in_specs=[page_spec(p) for p in range(npage)]
        + [pl.BlockSpec(wc.shape, lambda b, s, pt: (0, 0, 0))],
        out_specs=pl.BlockSpec((None, 2, G_B, rows, wc.shape[2]), lambda b, s, pt: (b, 0, 0, s, 0)),
        scratch_shapes=[pltpu.VMEM((2, G_B * rows, CMP_STRIDE * HEAD_DIM), F32)],
    )
    return pl.pallas_call(
        functools.partial(_cmp_pages_kernel, npage=npage),
        grid_spec=grid_spec,
        out_shape=jax.ShapeDtypeStruct((Bs, 2, G_B, nch, wc.shape[2]), F32),
        compiler_params=_cparams(("parallel", "arbitrary")),
    )(page_table, *([cache] * npage), wc)


def _nsa_sample_cmp_kernel(sl_ref, part_ref, cb_ref, w2_ref, q_ref, msel_ref, ocmp_ref, selm_ref,
                           *, tp, hpg, nsel, ns, past):
    g = pl.program_id(1)
    slopes = [sl_ref[g * hpg + r] for r in range(hpg)]
    kc = _cmp_finish(part_ref[0], cb_ref[0], w2_ref[0]).astype(BF16)
    vc = _cmp_finish(part_ref[1], cb_ref[1], w2_ref[1]).astype(BF16)
    nch = kc.shape[0]
    qpos = past + lax.broadcasted_iota(jnp.int32, (tp, 1), 0)
    Q = _stack_heads(q_ref[...], hpg)
    idx = lax.broadcasted_iota(jnp.int32, (1, nch), 1)
    dist = qpos - (idx * CMP_STRIDE + (CMP_BLOCK - 1))
    mask = (dist >= 0) & (idx < nch - 1)
    sc = _dot_nt(Q, kc) * SCALE
    (pc,), o_cmp = _softmax_pieces([(sc, dist.astype(F32), mask, vc)], hpg, tp, slopes)
    imp = pc[0:tp]
    for r in range(1, hpg):
        imp = imp + pc[r * tp:(r + 1) * tp]
    ocmp_ref[...] = o_cmp
    selm_ref[...] = _sel_score_mask(imp, msel_ref[...], qpos, nsel, ns)


def _nsa_sample_cmp(u, part, cb, w2, Bs, tp, hpg, past):
    nch = part.shape[3]
    nc = nch - CMP_BLOCK // CMP_STRIDE + 1
    ns = past // SEL_BLOCK + 1
    nsp = -(-ns // 128) * 128
    nsel = min(N_SEL, ns)
    qw = hpg * HEAD_DIM
    msel = _msel_matrix(nch, nsp, nc, ns)
    return pl.pallas_call(
        functools.partial(_nsa_sample_cmp_kernel, tp=tp, hpg=hpg, nsel=nsel, ns=ns, past=past),
        grid=(Bs, G_B),
        in_specs=[pl.BlockSpec(memory_space=pltpu.SMEM),
                  pl.BlockSpec((None, 2, None, nch, part.shape[4]), lambda b, g: (b, 0, g, 0, 0)),
                  pl.BlockSpec(cb.shape, lambda b, g: (0, 0, 0)),
                  pl.BlockSpec(w2.shape, lambda b, g: (0, 0, 0)),
                  pl.BlockSpec((None, tp, qw), lambda b, g: (b, 0, g)),
                  pl.BlockSpec(msel.shape, lambda b, g: (0, 0))],
        out_specs=[pl.BlockSpec((None, None, hpg * tp, HEAD_DIM), lambda b, g: (b, g, 0, 0)),
                   pl.BlockSpec((None, None, tp, nsp), lambda b, g: (b, g, 0, 0))],
        out_shape=[jax.ShapeDtypeStruct((Bs, G_B, hpg * tp, HEAD_DIM), F32),
                   jax.ShapeDtypeStruct((Bs, G_B, tp, nsp), F32)],
        compiler_params=_cparams(("parallel", "parallel")),
    )(_alibi_slopes(G_B * hpg), part, cb, w2, u, msel)


def _nsa_sample_kernel(pt_ref, lp_ref, cnt_ref, sl_ref, *refs, npage, tp, hpg, past):
    page_refs = refs[:npage]
    (q_ref, gt_ref, selp_ref, selm_ref, ocmp_ref, kvn_ref, cwin_ref, e_ref, o_ref,
     m_ref, l_ref, acc_ref) = refs[npage:]
    b_id = pl.program_id(0)
    s_id = pl.program_id(1)
    nsteps = pl.num_programs(1)
    tk = npage * PAGE_SIZE
    qpos = past + lax.broadcasted_iota(jnp.int32, (tp, 1), 0)

    @pl.when(s_id == 0)
    def _():
        m_ref[...] = jnp.full(m_ref.shape, NEG, F32)
        l_ref[...] = jnp.zeros(l_ref.shape, F32)
        acc_ref[...] = jnp.zeros(acc_ref.shape, F32)

    @pl.when(s_id * npage < cnt_ref[b_id])
    def _():
        lane = lax.broadcasted_iota(jnp.int32, (1, PAGE_SIZE), 1)
        kpos = jnp.concatenate([lp_ref[b_id, s_id * npage + p] * PAGE_SIZE + lane for p in range(npage)], axis=1)
        d = qpos - kpos
        df = d.astype(F32)
        for g in range(G_B):
            slopes = [sl_ref[g * hpg + r] for r in range(hpg)]
            Q = _stack_heads(q_ref[:, g * hpg * HEAD_DIM:(g + 1) * hpg * HEAD_DIM], hpg)
            slabs = page_refs[0].shape[0] // PAGE_SIZE
            kb = jnp.concatenate([page_refs[p][pl.ds(2 * G_B + g, PAGE_SIZE, stride=slabs), :]
                                  for p in range(npage)], axis=0).astype(BF16)
            vb = jnp.concatenate([page_refs[p][pl.ds(3 * G_B + g, PAGE_SIZE, stride=slabs), :]
                                  for p in range(npage)], axis=0).astype(BF16)
            s = _dot_nt(Q, kb) * SCALE
            mask = (_dot_fewrows(selp_ref[g], e_ref[...]) > 0.5) & (d >= 0)
            ps, alphas = [], []
            for r in range(hpg):
                hr = slice(r * tp, (r + 1) * tp)
                sr = jnp.where(mask, s[hr] - slopes[r] * df, NEG)
                m_old = m_ref[g, hr]
                m_new = jnp.maximum(m_old, sr.max(axis=-1, keepdims=True))
                alpha = jnp.exp(m_old - m_new)
                p = jnp.where(mask, jnp.exp(sr - m_new), 0.0)
                m_ref[g, hr] = m_new
                l_ref[g, hr] = alpha * l_ref[g, hr] + p.sum(axis=-1, keepdims=True)
                ps.append(p)
                alphas.append(jnp.broadcast_to(alpha, (tp, HEAD_DIM)))
            pv = _dot(jnp.concatenate(ps, axis=0).astype(BF16), vb)
            acc_ref[g] = jnp.concatenate(alphas, axis=0) * acc_ref[g] + pv

    @pl.when(s_id == nsteps - 1)
    def _():
        gt = _sigmoid(gt_ref[...])
        npad = HEAD_DIM
        kidx = lax.broadcasted_iota(jnp.int32, (1, npad), 1)
        dn = qpos - (past + kidx)
        new_ok = (kidx < tp) & (dn >= 0)
        zpad = jnp.zeros((npad - tp, HEAD_DIM), F32)
        nblk = past // SEL_BLOCK
        wpast = cwin_ref.shape[0] // (2 * G_B)
        dwp = qpos - (past - wpast + lax.broadcasted_iota(jnp.int32, (1, wpast), 1))
        for g in range(G_B):
            slopes = [sl_ref[g * hpg + r] for r in range(hpg)]
            Q = _stack_heads(q_ref[:, g * hpg * HEAD_DIM:(g + 1) * hpg * HEAD_DIM], hpg)

            def newrows(c):
                blk = kvn_ref[:, (c * G_B + g) * HEAD_DIM:(c * G_B + g + 1) * HEAD_DIM]
                return jnp.concatenate([blk, zpad], axis=0).astype(BF16)

            kn, vn = newrows(2), newrows(3)
            sn = _dot_nt(Q, kn) * SCALE
            mask_n = new_ok & (selm_ref[g][:, nblk:nblk + 1] > 0.5)
            o_sel = []
            for r in range(hpg):
                hr = slice(r * tp, (r + 1) * tp)
                sr = jnp.where(mask_n, sn[hr] - slopes[r] * dn.astype(F32), NEG)
                m_old = m_ref[g, hr]
                m_new = jnp.maximum(m_old, sr.max(axis=-1, keepdims=True))
                alpha = jnp.exp(m_old - m_new)
                p = jnp.where(mask_n, jnp.exp(sr - m_new), 0.0)
                l = alpha * l_ref[g, hr] + p.sum(axis=-1, keepdims=True)
                acc = alpha * acc_ref[g, hr] + _dot_fewrows(p, vn)
                o_sel.append(acc / jnp.maximum(l, 1e-30))
            kwp = cwin_ref[pl.ds(g, wpast, stride=2 * G_B), :].astype(BF16)
            vwp = cwin_ref[pl.ds(G_B + g, wpast, stride=2 * G_B), :].astype(BF16)
            kwn, vwn = newrows(4), newrows(5)
            pieces = [(_dot_nt(Q, kwp) * SCALE, dwp.astype(F32), (dwp >= 0) & (dwp < WINDOW), vwp),
                      (_dot_nt(Q, kwn) * SCALE, dn.astype(F32), new_ok & (dn < WINDOW), vwn)]
            _, o_win = _softmax_pieces(pieces, hpg, tp, slopes)
            o_cmp = ocmp_ref[g]
            for r in range(hpg):
                hr = slice(r * tp, (r + 1) * tp)
                c = 3 * (g * hpg + r)
                col = (g * hpg + r) * HEAD_DIM
                o_ref[:, col:col + HEAD_DIM] = (gt[:, c:c + 1] * o_cmp[hr] + gt[:, c + 1:c + 2] * o_sel[r]
                                                + gt[:, c + 2:c + 3] * o_win[hr])


def _nsa_sample(u, cache, page_table, selm, o_cmp, kvn, cwin, Bs, tp, ts, hpg, past):
    npg = page_table.shape[1]
    npage = SEL_PAGES
    assert npg % npage == 0
    nsteps = npg // npage
    tk = npage * PAGE_SIZE
    nsp = selm.shape[3]
    qw = G_B * hpg * HEAD_DIM
    bpp = PAGE_SIZE // SEL_BLOCK
    e3 = _expand_matrix(nsteps, bpp * npg, tk)
    selm = jnp.where((jnp.arange(tp) < ts)[None, None, :, None], selm, 0.0)
    blocks = selm[..., :bpp * npg].reshape(Bs, G_B, tp, npg, bpp)
    need = jnp.max(blocks, axis=(1, 2, 4)) > 0.5
    order = jnp.argsort(jnp.logical_not(need), axis=1, stable=True).astype(jnp.int32)
    cnt = jnp.sum(need, axis=1).astype(jnp.int32)
    slot = jnp.arange(npg, dtype=jnp.int32)[None]
    last = jnp.take_along_axis(order, jnp.maximum(cnt - 1, 0)[:, None], axis=1)
    lpage = jnp.where(slot < cnt[:, None], order, last)
    ppage = jnp.take_along_axis(page_table, lpage, axis=1)
    selp = jnp.take_along_axis(blocks, lpage[:, None, None, :, None], axis=3)
    selp = jnp.where((slot < cnt[:, None])[:, None, None, :, None], selp, 0.0).reshape(Bs, G_B, tp, bpp * npg)
    gates = jnp.concatenate([u[:, :, qw + MEM_WIDTH + g * HEAD_DIM:qw + MEM_WIDTH + g * HEAD_DIM + 3 * hpg]
                             for g in range(G_B)]
                            + [jnp.zeros((Bs, tp, HEAD_DIM - 3 * hpg * G_B), F32)], axis=2)

    def page_spec(p):
        return pl.BlockSpec((None,) + cache.shape[1:], lambda b, s, pt, lp, cn: (pt[b, s * npage + p], 0, 0))

    def per_batch(shape):
        zeros = (0,) * len(shape)
        return pl.BlockSpec((None,) + shape, lambda b, s, pt, lp, cn: (b,) + zeros)

    grid_spec = pltpu.PrefetchScalarGridSpec(
        num_scalar_prefetch=3,
        grid=(Bs, nsteps),
        in_specs=[pl.BlockSpec(memory_space=pltpu.SMEM)]
        + [page_spec(p) for p in range(npage)]
        + [per_batch((tp, qw)), per_batch((tp, HEAD_DIM)), per_batch((G_B, tp, bpp * npg)),
           per_batch((G_B, tp, nsp)), per_batch((G_B, hpg * tp, HEAD_DIM)), per_batch((tp, kvn.shape[2])),
           per_batch(cwin.shape[1:]),
           pl.BlockSpec((None, bpp * npg, tk), lambda b, s, pt, lp, cn: (s, 0, 0))],
        out_specs=per_batch((tp, qw)),
        scratch_shapes=[pltpu.VMEM((G_B, hpg * tp, 1), F32), pltpu.VMEM((G_B, hpg * tp, 1), F32),
                        pltpu.VMEM((G_B, hpg * tp, HEAD_DIM), F32)],
    )
    return pl.pallas_call(
        functools.partial(_nsa_sample_kernel, npage=npage, tp=tp, hpg=hpg, past=past),
        grid_spec=grid_spec,
        out_shape=jax.ShapeDtypeStruct((Bs, tp, qw), F32),
        compiler_params=_cparams(("parallel", "arbitrary")),
    )(ppage, lpage, cnt, _alibi_slopes(G_B * hpg), *([cache] * npage), u, gates, selp, selm, o_cmp, kvn, cwin, e3)


def _prep_weights(w_in_a, w_in_b, w_o, w_mem_kv, w_kv_b, w_cmp1, w_cmp2, w_ffn_in, w_ffn_out, cmp_pos, hpg):
    tokw = w_in_b.shape[2] - 3 * G_B * hpg - MEM_WIDTH
    D = w_in_b.shape[1]
    wb = w_in_b[0]
    gates = wb[:, tokw:tokw + 3 * G_B * hpg]
    gpad = jnp.zeros((D, HEAD_DIM - 3 * hpg), F32)
    wb2 = jnp.concatenate([wb[:, :tokw], wb[:, tokw + 3 * G_B * hpg:]]
                          + sum([[gates[:, g * 3 * hpg:(g + 1) * 3 * hpg], gpad] for g in range(G_B)], []),
                          axis=1)
    r = CMP_BLOCK // CMP_STRIDE
    flat = CMP_STRIDE * HEAD_DIM
    wc = w_cmp1.reshape(2, r, flat, HEAD_DIM).transpose(0, 2, 1, 3).reshape(2, flat, r * HEAD_DIM)
    pe2 = jnp.concatenate([cmp_pos.reshape(2, r, flat), jnp.zeros((2, 16 - r, flat), F32)], axis=1)
    L, _, F2 = w_ffn_in.shape
    npiece = F2 // 2 // FFN_PIECE
    w_ffn_in = w_ffn_in.reshape(L, D, 2, npiece, FFN_PIECE).transpose(0, 1, 3, 2, 4).reshape(L, D, F2)
    return dict(w_in_a=w_in_a.astype(BF16), w_in_b=wb2.astype(BF16), w_o=w_o.astype(BF16),
                w_mem_kv=w_mem_kv.astype(BF16), w_kv_b=w_kv_b.astype(BF16), wc=wc.astype(BF16),
                w2=w_cmp2.astype(BF16), w_ffn_in=w_ffn_in.astype(BF16), w_ffn_out=w_ffn_out.astype(BF16),
                pe2=pe2)


def _layer_tail(h, mix, mo, l, prm, conv_buf, B, T):
    g = prm['norm_gains'][l]
    h, xn = _mmres(mix, mo, (prm['w_o'], l), h, g[1], g_next=g[2])
    y, new_buf = _ffn_in(xn, (prm['w_ffn_in'], l), prm['w_ffn_conv'][l], prm['b_ffn_conv'][l],
                         None if conv_buf is None else conv_buf[l], B, T)
    h = _mmres(y, None, (prm['w_ffn_out'], l), h, g[3])
    return h, new_buf


def _trunk(x, mem_kv, hgrn_s0, conv_buf, prm, B, T, t_valid, nsa_fn):
    D = x.shape[-1]
    H = prm['hgrn_norm'].shape[1]
    h = x.reshape(B * T, D)
    u = _mm(h, prm['norm_gains'][0, 0], (prm['w_in_a'], 0)).reshape(B, T, -1)
    mix, s_fin = _hgrn(u, prm['lb_logits'], prm['hgrn_norm'][0], None if hgrn_s0 is None else hgrn_s0[0],
                       B, T, H, t_valid)
    mo = _memattn(u, 4 * H * HEAD_DIM // MEM_WIDTH, mem_kv, 0, B, T)
    h, buf0 = _layer_tail(h, mix.reshape(B * T, -1), mo.reshape(B * T, -1), 0, prm, conv_buf, B, T)
    n_slab = prm['w_kv_b'].shape[1] // HEAD_DIM
    kv, kv_rows, kv_win = _mm(h, prm['kv_norm'], prm['w_kv_b'], slabs=(n_slab - 2 * G_B, 2 * G_B))
    u = _mm(h, prm['norm_gains'][1, 0], prm['w_in_b'])
    mix = nsa_fn(u, kv)
    mo = _memattn(u.reshape(B, T, -1), H * HEAD_DIM // MEM_WIDTH, mem_kv, 1, B, T)
    h, buf1 = _layer_tail(h, mix.reshape(B * T, -1), mo.reshape(B * T, -1), 1, prm, conv_buf, B, T)
    return h.reshape(B, T, D), s_fin, (buf0, buf1), kv_rows, kv_win


def kernel(x_prompt, x_sample, mem_prompt, state_hgrn, cache_conv, cache_mem, cache_kv, cache_win,
           page_table, norm_gains, w_in_a, lb_logits, hgrn_norm, w_in_b, w_o, w_mem_kv, kv_norm,
           w_kv_b, cmp_pos, w_cmp1, w_cmp2, w_ffn_in, w_ffn_conv, b_ffn_conv, w_ffn_out):
    Bp, Tp, D = x_prompt.shape
    Bs, Ts, _ = x_sample.shape
    depth = norm_gains.shape[0]
    assert depth == 2 and w_in_a.shape[0] == 1 and w_in_b.shape[0] == 1
    H = hgrn_norm.shape[1]
    hpg = H // G_B
    ml = mem_prompt.shape[1]
    n_rows = cache_kv.shape[2]

    prm = _prep_weights(w_in_a, w_in_b, w_o, w_mem_kv, w_kv_b, w_cmp1, w_cmp2, w_ffn_in, w_ffn_out,
                        cmp_pos, hpg)
    prm.update(norm_gains=norm_gains, lb_logits=lb_logits, hgrn_norm=hgrn_norm, kv_norm=kv_norm,
               w_ffn_conv=w_ffn_conv, b_ffn_conv=b_ffn_conv)
    cb = _cmp_bias(prm['pe2'], prm['wc'])

    memx = mem_prompt.reshape(Bp * ml, D)
    mem_kv_p = [_mm(memx, None, (prm['w_mem_kv'], l), norm=False).reshape(Bp, ml, 2 * MEM_WIDTH)
                for l in range(depth)]

    def nsa_p(u, kv):
        kcv = _cmp_prompt(kv, prm['wc'], prm['w2'], cb, Bp, Tp)
        return _nsa_prompt(u, kv, kcv, Bp, Tp, hpg)

    y_p, hgrn_p, conv_p, kvr_p, kvw_p = _trunk(x_prompt, mem_kv_p, None, None, prm, Bp, Tp, Tp, nsa_p)

    tp = -(-Ts // SUBLANES) * SUBLANES
    past = page_table.shape[1] * PAGE_SIZE
    assert Ts < CMP_STRIDE and tp <= SEL_BLOCK and past >= WINDOW
    xs = jnp.concatenate([x_sample, jnp.zeros((Bs, tp - Ts, D), F32)], axis=1)

    assert n_rows == 4
    cache5 = cache_kv.reshape(cache_kv.shape[0], PAGE_SIZE * n_rows * G_B, HEAD_DIM)

    def nsa_s(u, kv):
        u3 = u.reshape(Bs, tp, -1)
        part = _cmp_pages(cache5, page_table, prm['wc'])
        o_cmp, selm = _nsa_sample_cmp(u3, part, cb, prm['w2'], Bs, tp, hpg, past)
        return _nsa_sample(u3, cache5, page_table, selm, o_cmp, kv.reshape(Bs, tp, -1),
                           cache_win.reshape(Bs, -1, HEAD_DIM), Bs, tp, Ts, hpg, past)

    y_s, hgrn_s, conv_s, kvr_s, kvw_s = _trunk(xs, cache_mem.reshape(depth, Bs, -1, HEAD_DIM), state_hgrn, cache_conv,
                                       prm, Bs, tp, Ts, nsa_s)

    wl = min(WINDOW, Tp)
    return (
        y_p,
        y_s[:, :Ts],
        hgrn_p[None],
        hgrn_s[None],
        jnp.stack(conv_p),
        jnp.stack([c[:, Ts - (CONV_W - 1):Ts] for c in conv_s]),
        jnp.stack(mem_kv_p).reshape(depth, Bp, ml, 2, MEM_HEADS, HEAD_DIM),
        kvr_p.reshape(Bp, Tp // PAGE_SIZE, PAGE_SIZE, n_rows, G_B, HEAD_DIM),
        kvr_s.reshape(Bs, tp, n_rows, G_B, HEAD_DIM)[:, :Ts],
        kvw_p.reshape(Bp, Tp, 2, G_B, HEAD_DIM)[:, Tp - wl:],
        kvw_s.reshape(Bs, tp, 2, G_B, HEAD_DIM)[:, :Ts],
    )
```

```python
import functools
import math

import ml_dtypes
import numpy as np
import jax
import jax.numpy as jnp
from jax import lax
from jax.experimental import pallas as pl
from jax.experimental.pallas import tpu as pltpu

F32 = jnp.float32
BF16 = jnp.bfloat16

HEAD_DIM = 128
MEM_HEADS = 4
MEM_WIDTH = MEM_HEADS * HEAD_DIM
G_B = 2
HGRN_CHUNK = 32
CMP_BLOCK = 32
CMP_STRIDE = 16
SEL_BLOCK = 64
N_SEL = 16
WINDOW = 512
PAGE_SIZE = 128
FORCE_BONUS = 1e4
CONV_W = 3
EPS = 1e-6
NEG = -1e30
MASK_BF16 = -2.0 ** 100
POS_LANE = 120
SCALE = HEAD_DIM ** -0.5
SUBLANES = 8
FFN_PIECE = 256
HGRN_HEADS = 6
VMEM_LIMIT = 56 * 1024 * 1024


def _cparams(sem):
    return pltpu.CompilerParams(dimension_semantics=sem, vmem_limit_bytes=VMEM_LIMIT)


def _dot(a, b):
    return jnp.dot(a, b, preferred_element_type=F32)


def _dot_nt(a, b):
    return lax.dot_general(a, b, (((1,), (1,)), ((), ())), preferred_element_type=F32)


def _dot_tn(a, b):
    return lax.dot_general(a, b, (((0,), (0,)), ((), ())), preferred_element_type=F32)


BF16_ROWS = 16


def _pad_rows(x):
    n = x.shape[0]
    if n % BF16_ROWS == 0:
        return x
    return jnp.concatenate([x, jnp.zeros((BF16_ROWS - n % BF16_ROWS,) + x.shape[1:], x.dtype)], axis=0)


def _dot_fewrows(a, b):
    return _dot(_pad_rows(a).astype(BF16), b)[:a.shape[0]]


def _rms(x, g):
    return x * lax.rsqrt(jnp.mean(x * x, axis=-1, keepdims=True) + EPS) * g


def _sigmoid(x):
    return 0.5 * jnp.tanh(0.5 * x) + 0.5


def _gelu(x):
    return 0.5 * x * (1.0 + jnp.tanh(0.7978845608028654 * (x + 0.044715 * (x * x * x))))


def _layer_weight(w, block, index_map):
    if isinstance(w, tuple):
        w3, l = w
        return w3, pl.BlockSpec((None,) + block, lambda *a: (l,) + index_map(*a))
    return w, pl.BlockSpec(block, index_map)


def _wshape(w):
    return w[0].shape[1:] if isinstance(w, tuple) else w.shape


def _pick(n, cands):
    for c in cands:
        if n % c == 0:
            return c
    raise ValueError(f"no tile for {n} in {cands}")


def _mm_kernel(x_ref, g_ref, w_ref, o_ref, *rest, norm, rows, slabs):
    xn_ref = rest[-1]
    tm = x_ref.shape[0]
    j = pl.program_id(1)

    @pl.when(j == 0)
    def _():
        def body(c, carry):
            r = pl.multiple_of(c * rows, rows)
            x = x_ref[pl.ds(r, rows), :]
            if norm:
                x = _rms(x, g_ref[...])
            xn_ref[pl.ds(r, rows), :] = x.astype(BF16)
            return carry
        lax.fori_loop(0, tm // rows, body, 0)

    res = _dot(xn_ref[...], w_ref[...])
    o_ref[...] = res
    if slabs:
        per_step = res.shape[1] // HEAD_DIM
        first = 0
        for s_ref, nslab in zip(rest[:-1], slabs):
            steps = nslab // per_step

            @pl.when((j >= first) & (j < first + steps))
            def _(s_ref=s_ref, nslab=nslab, first=first):
                for c in range(per_step):
                    s_ref[pl.ds((j - first) * per_step + c, tm, stride=nslab), :] = (
                        res[:, c * HEAD_DIM:(c + 1) * HEAD_DIM])
            first += steps


def _mm(x, g, w, *, norm=True, slabs=()):
    M, K = x.shape
    N = _wshape(w)[1]
    tm = _pick(M, (1024, 512, 256, 128, 64, 32, 16))
    tn = _pick(N, (1664, 1152, 512, 256, 128)) if not slabs else _pick(N, (512, 256, 128))
    w, w_spec = _layer_weight(w, (K, tn), lambda i, j: (0, j))
    rows = min(tm, 128)
    if g is None:
        g = jnp.ones((K,), F32)
    out_specs = [pl.BlockSpec((tm, tn), lambda i, j: (i, j))]
    out_shape = [jax.ShapeDtypeStruct((M, N), F32)]
    for n in slabs:
        assert (n * HEAD_DIM) % tn == 0
        out_specs.append(pl.BlockSpec((tm * n, HEAD_DIM), lambda i, j: (i, 0)))
        out_shape.append(jax.ShapeDtypeStruct((M * n, HEAD_DIM), F32))
    assert sum(slabs) in (0, N // HEAD_DIM)
    out = pl.pallas_call(
        functools.partial(_mm_kernel, norm=norm, rows=rows, slabs=tuple(slabs)),
        grid=(M // tm, N // tn),
        in_specs=[pl.BlockSpec((tm, K), lambda i, j: (i, 0)),
                  pl.BlockSpec((1, K), lambda i, j: (0, 0)),
                  w_spec],
        out_specs=out_specs,
        out_shape=out_shape,
        scratch_shapes=[pltpu.VMEM((tm, K), BF16)],
        compiler_params=_cparams(("parallel", "arbitrary")),
    )(x, g.reshape(1, K), w)
    return out if slabs else out[0]


def _mmres_kernel(*refs, nk, two, nxt):
    refs = list(refs)
    a1_ref = refs.pop(0)
    a = a1_ref[...].astype(BF16)
    if two:
        a = jnp.concatenate([a, refs.pop(0)[...].astype(BF16)], axis=1)
    w_ref, h_ref, g_ref = refs[:3]
    gn_ref = refs[3] if nxt else None
    o_ref = refs[4] if nxt else refs[3]
    xn_ref = refs[5] if nxt else None
    k = pl.program_id(1)
    part = _dot(a, w_ref[...])

    def finish(acc):
        hn = h_ref[...] + _rms(acc, g_ref[...])
        o_ref[...] = hn
        if nxt:
            xn_ref[...] = _rms(hn, gn_ref[...]).astype(BF16)

    if nk == 1:
        finish(part)
    else:
        @pl.when(k == 0)
        def _():
            o_ref[...] = part

        @pl.when((k > 0) & (k < nk - 1))
        def _():
            o_ref[...] += part

        @pl.when(k == nk - 1)
        def _():
            finish(o_ref[...] + part)


def _mmres(a1, a2, w, h, g, g_next=None):
    M, N = h.shape
    K = _wshape(w)[0]
    two = a2 is not None
    nk = 1 if two else 2
    tk = K // nk
    assert K % nk == 0 and tk % 128 == 0
    tm = _pick(M, (512, 256, 128, 64, 32, 16))
    if two:
        assert a1.shape[1] + a2.shape[1] == K
        in_specs = [pl.BlockSpec((tm, a1.shape[1]), lambda i, k: (i, 0)),
                    pl.BlockSpec((tm, a2.shape[1]), lambda i, k: (i, 0))]
        args = [a1, a2]
    else:
        in_specs = [pl.BlockSpec((tm, tk), lambda i, k: (i, k))]
        args = [a1]
    w, w_spec = _layer_weight(w, (tk, N), lambda i, k: (k, 0))
    in_specs += [w_spec,
                 pl.BlockSpec((tm, N), lambda i, k: (i, 0)),
                 pl.BlockSpec((1, N), lambda i, k: (0, 0))]
    args += [w, h, g.reshape(1, N)]
    out_specs = [pl.BlockSpec((tm, N), lambda i, k: (i, 0))]
    out_shape = [jax.ShapeDtypeStruct((M, N), F32)]
    nxt = g_next is not None
    if nxt:
        in_specs.append(pl.BlockSpec((1, N), lambda i, k: (0, 0)))
        args.append(g_next.reshape(1, N))
        out_specs.append(pl.BlockSpec((tm, N), lambda i, k: (i, 0)))
        out_shape.append(jax.ShapeDtypeStruct((M, N), BF16))
    out = pl.pallas_call(
        functools.partial(_mmres_kernel, nk=nk, two=two, nxt=nxt),
        grid=(M // tm, nk),
        in_specs=in_specs,
        out_specs=out_specs,
        out_shape=out_shape,
        compiler_params=_cparams(("parallel", "arbitrary")),
    )(*args)
    return out if nxt else out[0]


def _ffn_in_kernel(xn_ref, wa_ref, wb_ref, wc_ref, bc_ref, p1_ref, p2_ref,
                   y_ref, a_ref, carry_ref, *, T, rparts, cparts):
    tm = xn_ref.shape[0]
    i = pl.program_id(0)
    j = pl.program_id(1)
    wc = wc_ref[...]
    bc = bc_ref[...]
    if T >= tm:
        tiles_per_batch = T // tm

        @pl.when(i % tiles_per_batch == 0)
        def _():
            carry_ref[j] = p1_ref[...]

        prev = carry_ref[j]
        pr, pc = tm // rparts, FFN_PIECE
        assert cparts * pc == wa_ref.shape[1]
        row = lax.broadcasted_iota(jnp.int32, (pr, 1), 0)
        for ch in range(cparts):
            cs = slice(ch * pc, (ch + 1) * pc)
            older, newer = prev[SUBLANES - 2:SUBLANES - 1, cs], prev[SUBLANES - 1:SUBLANES, cs]
            for rh in range(rparts):
                rs = slice(rh * pr, (rh + 1) * pr)
                xn = xn_ref[rs, :]
                a = _dot(xn, wa_ref[:, cs])
                b = _dot(xn, wb_ref[:, cs])
                a1 = jnp.where(row == 0, newer, pltpu.roll(a, 1, 0))
                a2 = jnp.where(row == 0, older, jnp.where(row == 1, newer, pltpu.roll(a, 2, 0)))
                c = bc[:, cs] + a2 * wc[0:1, cs] + a1 * wc[1:2, cs] + a * wc[2:3, cs]
                y_ref[rs, cs] = (_gelu(c) * b).astype(BF16)
                older, newer = a[pr - 2:pr - 1], a[pr - 1:pr]
            carry_ref[j, :, cs] = a[pr - SUBLANES:pr]
            a_ref[:, cs] = a[pr - SUBLANES:pr]
    else:
        xn = xn_ref[...]
        a = _dot(xn, wa_ref[...])
        b = _dot(xn, wb_ref[...])
        tmod = lax.broadcasted_iota(jnp.int32, (tm, 1), 0) % T
        a1 = jnp.where(tmod == 0, p1_ref[...], pltpu.roll(a, 1, 0))
        a2 = jnp.where(tmod < 2, p2_ref[...], pltpu.roll(a, 2, 0))
        a_ref[...] = a
        c = bc + a2 * wc[0:1] + a1 * wc[1:2] + a * wc[2:3]
        y_ref[...] = (_gelu(c) * b).astype(BF16)


def _ffn_in(x, w, wconv, bconv, buf, B, T):
    M, K = x.shape
    F = _wshape(w)[1] // 2
    tn = 512
    nj = F // tn
    _, wa_spec = _layer_weight(w, (K, tn), lambda i, j: (0, j))
    w, wb_spec = _layer_weight(w, (K, tn), lambda i, j: (0, j + nj))
    assert F % tn == 0
    if buf is None:
        buf = jnp.zeros((B, CONV_W - 1, F), F32)
    wc = jnp.concatenate([wconv, jnp.zeros((SUBLANES - CONV_W, F), F32)], axis=0)
    if T >= 512:
        tm = _pick(T, (2048, 1024, 512))
        tpb = T // tm
        p1 = jnp.concatenate([jnp.zeros((B, SUBLANES - 2, F), F32), buf], axis=1)
        p2 = p1
        p_specs = [pl.BlockSpec((None, SUBLANES, tn), lambda i, j: (i // tpb, 0, j))] * 2
        a_shape = jax.ShapeDtypeStruct((M // tm, SUBLANES, F), F32)
        a_spec = pl.BlockSpec((None, SUBLANES, tn), lambda i, j: (i, 0, j))
    else:
        tm = M
        assert T >= 2 and M % T == 0
        z = jnp.zeros((B, 1, F), F32)
        p1 = jnp.concatenate([buf[:, 1:2]] + [z] * (T - 1), axis=1).reshape(M, F)
        p2 = jnp.concatenate([buf[:, 0:1], buf[:, 1:2]] + [z] * (T - 2), axis=1).reshape(M, F)
        p_specs = [pl.BlockSpec((tm, tn), lambda i, j: (i, j))] * 2
        a_shape = jax.ShapeDtypeStruct((M, F), F32)
        a_spec = pl.BlockSpec((tm, tn), lambda i, j: (i, j))
    y, a = pl.pallas_call(
        functools.partial(_ffn_in_kernel, T=T, rparts=max(tm // 1024, 1), cparts=tn // FFN_PIECE),
        grid=(M // tm, nj),
        in_specs=[pl.BlockSpec((tm, K), lambda i, j: (i, 0)),
                  wa_spec, wb_spec,
                  pl.BlockSpec((SUBLANES, tn), lambda i, j: (0, j)),
                  pl.BlockSpec((1, tn), lambda i, j: (0, j))] + p_specs,
        out_specs=[pl.BlockSpec((tm, tn), lambda i, j: (i, j)), a_spec],
        out_shape=[jax.ShapeDtypeStruct((M, F), BF16), a_shape],
        scratch_shapes=[pltpu.VMEM((nj, SUBLANES, tn), F32)],
        compiler_params=_cparams(("arbitrary", "arbitrary")),
    )(x, w, w, wc, bconv.reshape(1, F), p1, p2)
    if T >= 512:
        new_buf = a.reshape(B, T // tm, SUBLANES, F)[:, -1, SUBLANES - (CONV_W - 1):]
    else:
        new_buf = a.reshape(B, T, F)
    return y, new_buf


def _cumsum_tile(x):
    row = lax.broadcasted_iota(jnp.int32, (SUBLANES, 1), 0)
    d = 1
    while d < SUBLANES:
        x = x + jnp.where(row >= d, pltpu.roll(x, d, 0), 0.0)
        d *= 2
    return x


def _hgrn_chunk(uq, uf, v, ug, lb, gn, st, C, CP):
    nsub = CP // SUBLANES
    row = lax.broadcasted_iota(jnp.int32, (CP, 1), 0)
    row8 = row[0:SUBLANES]
    q = uq * _sigmoid(uq)
    fg = lb + (1.0 - lb) * _sigmoid(uf)
    kk = 1.0 - fg
    lf = jnp.log(fg)
    if C < CP:
        kk = jnp.where(row < C, kk, 0.0)
        lf = jnp.where(row < C, lf, 0.0)
    b_tiles, ends, off = [], [], None
    for i in range(nsub):
        bi = _cumsum_tile(lf[i * SUBLANES:(i + 1) * SUBLANES])
        if off is not None:
            bi = bi + off
        off = bi[SUBLANES - 1:SUBLANES]
        b_tiles.append(bi)
        ends.append(off)
    b = jnp.concatenate(b_tiles, axis=0) if nsub > 1 else b_tiles[0]
    b_last = off
    o = _dot_nt(_pad_rows(q * jnp.exp(b)).astype(BF16), st.astype(BF16))[:CP]
    if nsub > 1:
        e_all = jnp.concatenate([jnp.broadcast_to(e, (SUBLANES, HEAD_DIM)) for e in ends], axis=0)
        ke = kk * jnp.exp(e_all - b)
        lhs, rhs = [], []
        for j in range(nsub - 1):
            lo, hi = j * SUBLANES, (j + 1) * SUBLANES
            lhs.append(jnp.where(row >= hi, q * jnp.exp(jnp.minimum(b - ends[j], 0.0)), 0.0))
            rhs.append(jnp.where((row >= lo) & (row < hi), ke, 0.0))
        a_off = _dot_nt(jnp.concatenate(lhs, axis=1).astype(BF16), jnp.concatenate(rhs, axis=1).astype(BF16))
        o = o + _dot(a_off.astype(BF16), v.astype(BF16))
        kd = ke * jnp.exp(b_last - e_all)
    else:
        kd = kk * jnp.exp(b_last - b)
    o_tiles = []
    for i in range(nsub):
        sl = slice(i * SUBLANES, (i + 1) * SUBLANES)
        oi, qi, bi = o[sl], q[sl], b_tiles[i]
        for s in range(i * SUBLANES, min((i + 1) * SUBLANES, C)):
            d = jnp.exp(jnp.minimum(bi - b[s:s + 1], 0.0))
            a = jnp.sum(qi * kk[s:s + 1] * d, axis=-1, keepdims=True)
            oi = oi + jnp.where(row8 >= s - i * SUBLANES, a, 0.0) * v[s:s + 1]
        o_tiles.append(oi)
    o = jnp.concatenate(o_tiles, axis=0) if nsub > 1 else o_tiles[0]
    st_new = st * jnp.exp(b_last) + _dot_tn(_pad_rows(v).astype(BF16), _pad_rows(kd).astype(BF16))
    return _rms(o, gn) * _sigmoid(ug), st_new


def _hgrn_kernel(uq_ref, uf_ref, uv_ref, ug_ref, lbl_ref, gn_ref, s0_ref, o_ref, sfin_ref,
                 st_ref, *, C, CP, nchunk, hb):
    t = pl.program_id(2)

    @pl.when(t == 0)
    def _():
        for h in range(hb):
            st_ref[h] = s0_ref[h].T

    lbl = lbl_ref[...]
    e = jnp.exp(lbl - jnp.max(lbl, axis=0, keepdims=True))
    lb = e[0:1] / jnp.sum(e, axis=0, keepdims=True)
    gn = gn_ref[...]

    def chunk(c, carry):
        r0 = pl.multiple_of(c * CP, CP)
        for h in range(hb):
            cs = slice(h * HEAD_DIM, (h + 1) * HEAD_DIM)
            out, st_new = _hgrn_chunk(uq_ref[pl.ds(r0, CP), cs], uf_ref[pl.ds(r0, CP), cs],
                                      uv_ref[pl.ds(r0, CP), cs], ug_ref[pl.ds(r0, CP), cs],
                                      lb[:, cs], gn[:, cs], st_ref[h], C, CP)
            st_ref[h] = st_new
            o_ref[pl.ds(r0, CP), cs] = out
        return carry

    lax.fori_loop(0, nchunk, chunk, 0)

    @pl.when(t == pl.num_programs(2) - 1)
    def _():
        for h in range(hb):
            sfin_ref[h] = st_ref[h].T


def _hgrn(u, lb_logits, gnorm, s0, B, T, H, t_valid):
    C = math.gcd(t_valid, HGRN_CHUNK)
    assert C == t_valid or T == t_valid
    CP = max(C, SUBLANES)
    tt = _pick(T, (512, 256, 128, 64, 32, 16, 8))
    nchunk = tt // CP
    if s0 is None:
        s0 = jnp.zeros((B, H, HEAD_DIM, HEAD_DIM), F32)
    W = H * HEAD_DIM
    hb = _pick(H, (HGRN_HEADS, 2, 1))
    hg = H // hb
    wb = hb * HEAD_DIM

    def col(part):
        return pl.BlockSpec((None, tt, wb), lambda b, h, t: (b, t, part * hg + h))

    return pl.pallas_call(
        functools.partial(_hgrn_kernel, C=C, CP=CP, nchunk=nchunk, hb=hb),
        grid=(B, hg, T // tt),
        in_specs=[col(0), col(1), col(2), col(3),
                  pl.BlockSpec((lb_logits.shape[0], wb), lambda b, h, t: (0, h)),
                  pl.BlockSpec((1, wb), lambda b, h, t: (0, h)),
                  pl.BlockSpec((None, hb, HEAD_DIM, HEAD_DIM), lambda b, h, t: (b, h, 0, 0))],
        out_specs=[pl.BlockSpec((None, tt, wb), lambda b, h, t: (b, t, h)),
                   pl.BlockSpec((None, hb, HEAD_DIM, HEAD_DIM), lambda b, h, t: (b, h, 0, 0))],
        out_shape=[jax.ShapeDtypeStruct((B, T, W), F32),
                   jax.ShapeDtypeStruct((B, H, HEAD_DIM, HEAD_DIM), F32)],
        scratch_shapes=[pltpu.VMEM((hb, HEAD_DIM, HEAD_DIM), F32)],
        compiler_params=_cparams(("parallel", "parallel", "arbitrary")),
    )(u, u, u, u, lb_logits, gnorm.reshape(1, W), s0)


def _memattn_kernel(q_ref, kv_ref, o_ref):
    for h in range(MEM_HEADS):
        sl = slice(h * HEAD_DIM, (h + 1) * HEAD_DIM)
        q = _pad_rows(q_ref[:, sl]).astype(BF16)
        if kv_ref.shape[1] == HEAD_DIM:
            ml = kv_ref.shape[0] // (2 * MEM_HEADS)
            k = kv_ref[pl.ds(h, ml, stride=2 * MEM_HEADS), :].astype(BF16)
            v = kv_ref[pl.ds(MEM_HEADS + h, ml, stride=2 * MEM_HEADS), :].astype(BF16)
        else:
            k = kv_ref[:, sl].astype(BF16)
            v = kv_ref[:, MEM_WIDTH + h * HEAD_DIM:MEM_WIDTH + (h + 1) * HEAD_DIM].astype(BF16)
        s = _dot_nt(q, k) * SCALE
        p = jnp.exp(s - jnp.max(s, axis=-1, keepdims=True))
        l = jnp.sum(p, axis=-1, keepdims=True)
        o_ref[:, sl] = (_dot(p.astype(BF16), v) / l)[:q_ref.shape[0]]


def _memattn(u, qblock, mkv, layer, B, T):
    tq = _pick(T, (512, 256, 128, 64, 32, 16, 8))
    if isinstance(mkv, (list, tuple)):
        mkv = mkv[layer]
        kv_spec = pl.BlockSpec((None,) + mkv.shape[1:], lambda b, t: (b, 0, 0))
    else:
        kv_spec = pl.BlockSpec((None, None) + mkv.shape[2:], lambda b, t: (layer, b, 0, 0))
    return pl.pallas_call(
        _memattn_kernel,
        grid=(B, T // tq),
        in_specs=[pl.BlockSpec((None, tq, MEM_WIDTH), lambda b, t: (b, t, qblock)), kv_spec],
        out_specs=pl.BlockSpec((None, tq, MEM_WIDTH), lambda b, t: (b, t, 0)),
        out_shape=jax.ShapeDtypeStruct((B, T, MEM_WIDTH), F32),
        compiler_params=_cparams(("parallel", "parallel")),
    )(u, mkv)


def _cmp_bias_kernel(pe_ref, wc_ref, o_ref):
    o_ref[...] = _dot(pe_ref[...].astype(BF16), wc_ref[...])


def _cmp_bias(pe2, wc):
    return pl.pallas_call(
        _cmp_bias_kernel,
        grid=(2,),
        in_specs=[pl.BlockSpec((None, 16, pe2.shape[2]), lambda r: (r, 0, 0)),
                  pl.BlockSpec((None,) + wc.shape[1:], lambda r: (r, 0, 0))],
        out_specs=pl.BlockSpec((None, 16, wc.shape[2]), lambda r: (r, 0, 0)),
        out_shape=jax.ShapeDtypeStruct((2, 16, wc.shape[2]), F32),
    )(pe2, wc)


def _cmp_finish(part, cb, w2):
    nch = part.shape[0]
    bias = cb[0:1, 0:HEAD_DIM] + cb[1:2, HEAD_DIM:2 * HEAD_DIM]
    pre = bias + part[:, 0:HEAD_DIM] + pltpu.roll(part[:, HEAD_DIM:2 * HEAD_DIM], nch - 1, 0)
    return _dot(_gelu(pre).astype(BF16), w2)


def _cmp_prompt_kernel(kv0_ref, kv1_ref, wc_ref, w2_ref, cb_ref, o_ref, x_ref, *, nch):
    for g, kv_ref in enumerate((kv0_ref, kv1_ref)):
        for s in range(CMP_STRIDE):
            x = kv_ref[pl.ds(s, nch, stride=CMP_STRIDE), :]
            x_ref[g * nch:(g + 1) * nch, s * HEAD_DIM:(s + 1) * HEAD_DIM] = x.astype(BF16)
    part = _dot(x_ref[...], wc_ref[...])
    for g in range(G_B):
        o_ref[g] = _cmp_finish(part[g * nch:(g + 1) * nch], cb_ref[...], w2_ref[...])


def _cmp_prompt(kv, wc, w2, cb, B, T):
    nch = T // CMP_STRIDE
    assert nch % 16 == 0
    return pl.pallas_call(
        functools.partial(_cmp_prompt_kernel, nch=nch),
        grid=(B, 2),
        in_specs=[pl.BlockSpec((T, HEAD_DIM), lambda b, r: (b, G_B * r)),
                  pl.BlockSpec((T, HEAD_DIM), lambda b, r: (b, G_B * r + 1)),
                  pl.BlockSpec((None,) + wc.shape[1:], lambda b, r: (r, 0, 0)),
                  pl.BlockSpec((None, HEAD_DIM, HEAD_DIM), lambda b, r: (r, 0, 0)),
                  pl.BlockSpec((None, 16, wc.shape[2]), lambda b, r: (r, 0, 0))],
        out_specs=pl.BlockSpec((None, None, G_B, nch, HEAD_DIM), lambda b, r: (b, r, 0, 0, 0)),
        out_shape=jax.ShapeDtypeStruct((B, 2, G_B, nch, HEAD_DIM), F32),
        scratch_shapes=[pltpu.VMEM((G_B * nch, CMP_STRIDE * HEAD_DIM), BF16)],
        compiler_params=_cparams(("parallel", "parallel")),
    )(kv, kv, wc, w2, cb)


def _stack_heads(q, hpg):
    return jnp.concatenate([q[:, r * HEAD_DIM:(r + 1) * HEAD_DIM] for r in range(hpg)],
                           axis=0).astype(BF16)


def _softmax_pieces(pieces, hpg, tq, slopes):
    outs, probs = [], [[] for _ in pieces]
    for r in range(hpg):
        ss = []
        for (s, dist, mask, _) in pieces:
            sr = s[r * tq:(r + 1) * tq] - slopes[r] * dist
            ss.append(jnp.where(mask, sr, NEG))
        m = ss[0].max(axis=-1, keepdims=True)
        for sr in ss[1:]:
            m = jnp.maximum(m, sr.max(axis=-1, keepdims=True))
        es = [jnp.where(mask, jnp.exp(sr - m), 0.0) for sr, (_, _, mask, _) in zip(ss, pieces)]
        l = es[0].sum(axis=-1, keepdims=True)
        for ee in es[1:]:
            l = l + ee.sum(axis=-1, keepdims=True)
        inv = 1.0 / jnp.maximum(l, 1e-30)
        for i, ee in enumerate(es):
            probs[i].append(ee * inv)
    probs = [jnp.concatenate(p, axis=0) for p in probs]
    o = _dot(probs[0].astype(BF16), pieces[0][3])
    for p, piece in zip(probs[1:], pieces[1:]):
        o = o + _dot(p.astype(BF16), piece[3])
    return probs, o


def _sel_score_mask(imp, msel, qpos, nsel, ns):
    hi = imp.astype(BF16).astype(F32)
    mid = (imp - hi).astype(BF16).astype(F32)
    lo = imp - hi - mid
    p = _dot_fewrows(hi, msel) + _dot_fewrows(mid, msel) + _dot_fewrows(lo, msel)
    t, nsp = p.shape
    by_rows = t >= nsp and t % HEAD_DIM == 0
    if by_rows:
        nsr = -(-ns // SUBLANES) * SUBLANES
        p = p.T[:nsr]
        j = lax.broadcasted_iota(jnp.int32, (nsr, 1), 0)
        cur = (qpos[0:1] + lax.broadcasted_iota(jnp.int32, (1, t), 1)) // SEL_BLOCK
    else:
        j = lax.broadcasted_iota(jnp.int32, (1, nsp), 1)
        cur = qpos // SEL_BLOCK
    valid = (j <= cur) & (j < ns)
    forced = (j == 0) | (j == cur) | (j == cur - 1)
    score = jnp.where(valid, p + jnp.where(forced, FORCE_BONUS, 0.0), NEG)
    cnt = jnp.zeros(score.shape, F32)
    for i in range(ns):
        other = score[i:i + 1] if by_rows else score[:, i:i + 1]
        beats = (other > score) | ((other == score) & (i < j))
        cnt = cnt + jnp.where(beats, 1.0, 0.0)
    sel = jnp.where((cnt < nsel) & valid, 1.0, 0.0)
    if by_rows:
        sel = jnp.concatenate([sel, jnp.zeros((nsp - nsr, t), F32)], axis=0).T if nsr < nsp else sel.T
    return sel


def _nsa_prompt_kernel(sl_ref, q_ref, gt_ref, kc_ref, vc_ref, ks_ref, vs_ref, kw_ref, vw_ref,
                       msel_ref, paug_ref, srow_ref, o_ref, *, tq, tk, hpg, nsel, ns, T):
    g = pl.program_id(1)
    qi = pl.program_id(2)
    q0 = qi * tq
    slopes = [sl_ref[g * hpg + r] for r in range(hpg)]
    qpos = q0 + lax.broadcasted_iota(jnp.int32, (tq, 1), 0)
    lane = lax.broadcasted_iota(jnp.int32, (1, HEAD_DIM), 1)
    Q = _stack_heads(q_ref[...] * SCALE, hpg)

    ncp = kc_ref.shape[0]
    end = lax.broadcasted_iota(jnp.int32, (1, ncp), 1) * CMP_STRIDE + (CMP_BLOCK - 1)
    dist = qpos - end
    sc = _dot_nt(Q, kc_ref[...].astype(BF16))
    (pc,), o_cmp = _softmax_pieces([(sc, dist.astype(F32), dist >= 0, vc_ref[...].astype(BF16))],
                                   hpg, tq, slopes)
    imp = pc[0:tq]
    for r in range(1, hpg):
        imp = imp + pc[r * tq:(r + 1) * tq]
    selm = _sel_score_mask(imp, msel_ref[...], qpos, nsel, ns)

    srows = [srow_ref[pl.ds(g * hpg + r, 1), :] for r in range(hpg)]
    selneg = jnp.where((selm > 0.5) | (lane >= ns), 0.0, MASK_BF16)
    q_sel = jnp.concatenate(
        [Q, jnp.concatenate([selneg + sr for sr in srows], axis=0).astype(BF16)], axis=1)
    q_win = jnp.concatenate(
        [Q, jnp.concatenate([jnp.broadcast_to(sr, (tq, HEAD_DIM)) for sr in srows], axis=0).astype(BF16)],
        axis=1)
    ones_col = jnp.where(lane == 0, 1.0, 0.0).astype(BF16)

    def attend_tile(q_aug, k_ref, v_ref, k0, n, bias, ms, acc):
        k_aug = jnp.concatenate([k_ref[pl.ds(k0, n), :].astype(BF16), paug_ref[pl.ds(k0, n), :]], axis=1)
        v_aug = jnp.concatenate([v_ref[pl.ds(k0, n), :].astype(BF16),
                                 jnp.broadcast_to(ones_col, (n, HEAD_DIM))], axis=1)
        s = _dot_nt(q_aug, k_aug)
        new_ms, ps, alphas = [], [], []
        for r in range(hpg):
            sr = s[r * tq:(r + 1) * tq]
            if bias is not None:
                sr = sr + bias
            m_new = jnp.maximum(ms[r], sr.max(axis=-1, keepdims=True))
            new_ms.append(m_new)
            ps.append(jnp.exp(sr - m_new))
            alphas.append(jnp.broadcast_to(jnp.exp(ms[r] - m_new), (tq, 2 * HEAD_DIM)))
        pv = _dot(jnp.concatenate(ps, axis=0).astype(BF16), v_aug)
        return tuple(new_ms), jnp.concatenate(alphas, axis=0) * acc + pv

    def finish(acc):
        return acc[:, :HEAD_DIM] / jnp.maximum(acc[:, HEAD_DIM:HEAD_DIM + 1], 1e-30)

    init = (tuple(jnp.full((tq, 1), NEG, F32) for _ in range(hpg)),
            jnp.zeros((hpg * tq, 2 * HEAD_DIM), F32))

    nkt = (q0 + tq + tk - 1) // tk

    def kt_body(kt, carry):
        return attend_tile(q_sel, ks_ref, vs_ref, pl.multiple_of(kt * tk, tk), tk, None, *carry)

    ms, acc = lax.fori_loop(0, nkt - 1, kt_body, init)
    k0 = pl.multiple_of((nkt - 1) * tk, tk)
    causal = jnp.where(qpos >= k0 + lax.broadcasted_iota(jnp.int32, (1, tk), 1), 0.0, NEG)
    _, acc = attend_tile(q_sel, ks_ref, vs_ref, k0, tk, causal, ms, acc)
    o_sel = finish(acc)

    wl = WINDOW + tq
    kstart = pl.multiple_of(jnp.maximum(q0 - WINDOW, 0), tq)
    dw = qpos - (kstart + lax.broadcasted_iota(jnp.int32, (1, wl), 1))
    band = jnp.where((dw >= 0) & (dw < WINDOW), 0.0, NEG)
    _, acc = attend_tile(q_win, kw_ref, vw_ref, kstart, wl, band, *init)
    o_win = finish(acc)

    gt = _sigmoid(gt_ref[...])
    for r in range(hpg):
        rows = slice(r * tq, (r + 1) * tq)
        o_ref[:, r * HEAD_DIM:(r + 1) * HEAD_DIM] = (
            gt[:, 3 * r:3 * r + 1] * o_cmp[rows] + gt[:, 3 * r + 1:3 * r + 2] * o_sel[rows]
            + gt[:, 3 * r + 2:3 * r + 3] * o_win[rows])


def _alibi_slopes_np(h_b):
    h = np.arange(1, h_b + 1, dtype=np.float32)
    return np.exp2(-8.0 * h / h_b).astype(np.float32)


def _alibi_slopes(h_b):
    return jnp.asarray(_alibi_slopes_np(h_b))


def _msel_matrix(ncp, nsp, nc, ns):
    i = np.arange(ncp)[:, None]
    j = np.arange(nsp)[None, :]
    r = SEL_BLOCK // CMP_STRIDE
    back = CMP_BLOCK // CMP_STRIDE - 1
    m = (i >= r * j - back) & (i <= r * j + r - 1) & (i < nc) & (j < ns)
    return jnp.asarray(m.astype(np.float32), dtype=BF16)


def _expand_matrix(nsteps, nsp, tk):
    s = np.arange(nsteps)[:, None, None]
    j = np.arange(nsp)[None, :, None]
    key = np.arange(tk)[None, None, :]
    return jnp.asarray(((s * tk + key) // SEL_BLOCK == j).astype(np.float32), dtype=BF16)


def _nsa_prompt(u, kv, kcv, B, T, hpg):
    tq, tk = 256, 512
    assert T % tk == 0 and T >= WINDOW + tq
    ncp = T // CMP_STRIDE
    nc = ncp - CMP_BLOCK // CMP_STRIDE + 1
    ns = -(-T // SEL_BLOCK)
    nsp = HEAD_DIM
    assert ns <= POS_LANE and T <= 256 * HEAD_DIM
    nsel = min(N_SEL, ns)
    nqt = T // tq
    qw = hpg * HEAD_DIM
    gate_block0 = (G_B * qw + MEM_WIDTH) // HEAD_DIM
    msel = _msel_matrix(ncp, nsp, nc, ns)
    kpos = np.arange(T)
    pa = np.zeros((T, HEAD_DIM), np.float32)
    pa[kpos, kpos // SEL_BLOCK] = 1.0
    pa[:, POS_LANE:POS_LANE + 3] = (kpos // HEAD_DIM * HEAD_DIM)[:, None]
    pa[:, POS_LANE + 3:POS_LANE + 6] = (kpos % HEAD_DIM)[:, None]
    paug = jnp.asarray(pa, dtype=BF16)
    sl = np.asarray(_alibi_slopes_np(G_B * hpg))
    s1 = sl.astype(ml_dtypes.bfloat16).astype(np.float32)
    s2 = (sl - s1).astype(ml_dtypes.bfloat16).astype(np.float32)
    s3 = (sl - s1 - s2).astype(ml_dtypes.bfloat16).astype(np.float32)
    sr = np.zeros((-(-G_B * hpg // SUBLANES) * SUBLANES, HEAD_DIM), np.float32)
    for c, piece in enumerate((s1, s2, s3, s1, s2, s3)):
        sr[:G_B * hpg, POS_LANE + c] = piece
    srow = jnp.asarray(sr)

    def kvcol(c):
        return pl.BlockSpec((T, HEAD_DIM), lambda b, g, t: (b, c + g))

    def cmpspec(r):
        return pl.BlockSpec((None, None, None, ncp, HEAD_DIM), lambda b, g, t: (b, r, g, 0, 0))

    return pl.pallas_call(
        functools.partial(_nsa_prompt_kernel, tq=tq, tk=tk, hpg=hpg, nsel=nsel, ns=ns, T=T),
        grid=(B, G_B, nqt),
        in_specs=[pl.BlockSpec(memory_space=pltpu.SMEM),
                  pl.BlockSpec((tq, qw), lambda b, g, t: (b * nqt + t, g)),
                  pl.BlockSpec((tq, HEAD_DIM), lambda b, g, t: (b * nqt + t, gate_block0 + g)),
                  cmpspec(0), cmpspec(1), kvcol(4), kvcol(6), kvcol(8), kvcol(10),
                  pl.BlockSpec(msel.shape, lambda b, g, t: (0, 0)),
                  pl.BlockSpec(paug.shape, lambda b, g, t: (0, 0)),
                  pl.BlockSpec(srow.shape, lambda b, g, t: (0, 0))],
        out_specs=pl.BlockSpec((tq, qw), lambda b, g, t: (b * nqt + t, g)),
        out_shape=jax.ShapeDtypeStruct((B * T, G_B * qw), F32),
        compiler_params=_cparams(("parallel", "parallel", "arbitrary")),
    )(_alibi_slopes(G_B * hpg), u, u, kcv, kcv, kv, kv, kv, kv, msel, paug, srow)


CMP_PAGES = 32
SEL_PAGES = 16


def _cmp_pages_kernel(pt_ref, *refs, npage):
    page_refs = refs[:npage]
    wc_ref, o_ref, x_ref = refs[npage:]
    cpp = PAGE_SIZE // CMP_STRIDE
    rows = npage * cpp
    slabs = page_refs[0].shape[0] // PAGE_SIZE
    for rt in range(2):
        for g in range(G_B):
            for p in range(npage):
                for s in range(CMP_STRIDE):
                    x_ref[rt, g * rows + p * cpp:g * rows + (p + 1) * cpp, s * HEAD_DIM:(s + 1) * HEAD_DIM] = (
                        page_refs[p][pl.ds(s * slabs + rt * G_B + g, cpp, stride=CMP_STRIDE * slabs), :])
    for rt in range(2):
        part = _dot(x_ref[rt].astype(BF16), wc_ref[rt])
        for g in range(G_B):
            o_ref[rt, g] = part[g * rows:(g + 1) * rows]


def _cmp_pages(cache, page_table, wc):
    Bs, npg = page_table.shape
    npage = CMP_PAGES
    assert npg % npage == 0
    cpp = PAGE_SIZE // CMP_STRIDE
    rows = npage * cpp
    nch = npg * cpp

    def page_spec(p):
        return pl.BlockSpec((None,) + cache.shape[1:], lambda b, s, pt: (pt[b, s * npage + p], 0, 0))

    grid_spec = pltpu.PrefetchScalarGridSpec(
        num_scalar_prefetch=1,
        grid=(Bs, npg // npage),
        in_specs=[page_spec(p) for p in range(npage)]
        + [pl.BlockSpec(wc.shape, lambda b, s, pt: (0, 0, 0))],
        out_specs=pl.BlockSpec((None, 2, G_B, rows, wc.shape[2]), lambda b, s, pt: (b, 0, 0, s, 0)),
        scratch_shapes=[pltpu.VMEM((2, G_B * rows, CMP_STRIDE * HEAD_DIM), F32)],
    )
    return pl.pallas_call(
        functools.partial(_cmp_pages_kernel, npage=npage),
        grid_spec=grid_spec,
        out_shape=jax.ShapeDtypeStruct((Bs, 2, G_B, nch, wc.shape[2]), F32),
        compiler_params=_cparams(("parallel", "arbitrary")),
    )(page_table, *([cache] * npage), wc)


def _nsa_sample_cmp_kernel(sl_ref, part_ref, cb_ref, w2_ref, q_ref, msel_ref, ocmp_ref, selm_ref,
                           *, tp, hpg, nsel, ns, past):
    g = pl.program_id(1)
    slopes = [sl_ref[g * hpg + r] for r in range(hpg)]
    kc = _cmp_finish(part_ref[0], cb_ref[0], w2_ref[0]).astype(BF16)
    vc = _cmp_finish(part_ref[1], cb_ref[1], w2_ref[1]).astype(BF16)
    nch = kc.shape[0]
    qpos = past + lax.broadcasted_iota(jnp.int32, (tp, 1), 0)
    Q = _stack_heads(q_ref[...], hpg)
    idx = lax.broadcasted_iota(jnp.int32, (1, nch), 1)
    dist = qpos - (idx * CMP_STRIDE + (CMP_BLOCK - 1))
    mask = (dist >= 0) & (idx < nch - 1)
    sc = _dot_nt(Q, kc) * SCALE
    (pc,), o_cmp = _softmax_pieces([(sc, dist.astype(F32), mask, vc)], hpg, tp, slopes)
    imp = pc[0:tp]
    for r in range(1, hpg):
        imp = imp + pc[r * tp:(r + 1) * tp]
    ocmp_ref[...] = o_cmp
    selm_ref[...] = _sel_score_mask(imp, msel_ref[...], qpos, nsel, ns)


def _nsa_sample_cmp(u, part, cb, w2, Bs, tp, hpg, past):
    nch = part.shape[3]
    nc = nch - CMP_BLOCK // CMP_STRIDE + 1
    ns = past // SEL_BLOCK + 1
    nsp = -(-ns // 128) * 128
    nsel = min(N_SEL, ns)
    qw = hpg * HEAD_DIM
    msel = _msel_matrix(nch, nsp, nc, ns)
    return pl.pallas_call(
        functools.partial(_nsa_sample_cmp_kernel, tp=tp, hpg=hpg, nsel=nsel, ns=ns, past=past),
        grid=(Bs, G_B),
        in_specs=[pl.BlockSpec(memory_space=pltpu.SMEM),
                  pl.BlockSpec((None, 2, None, nch, part.shape[4]), lambda b, g: (b, 0, g, 0, 0)),
                  pl.BlockSpec(cb.shape, lambda b, g: (0, 0, 0)),
                  pl.BlockSpec(w2.shape, lambda b, g: (0, 0, 0)),
                  pl.BlockSpec((None, tp, qw), lambda b, g: (b, 0, g)),
                  pl.BlockSpec(msel.shape, lambda b, g: (0, 0))],
        out_specs=[pl.BlockSpec((None, None, hpg * tp, HEAD_DIM), lambda b, g: (b, g, 0, 0)),
                   pl.BlockSpec((None, None, tp, nsp), lambda b, g: (b, g, 0, 0))],
        out_shape=[jax.ShapeDtypeStruct((Bs, G_B, hpg * tp, HEAD_DIM), F32),
                   jax.ShapeDtypeStruct((Bs, G_B, tp, nsp), F32)],
        compiler_params=_cparams(("parallel", "parallel")),
    )(_alibi_slopes(G_B * hpg), part, cb, w2, u, msel)


def _nsa_sample_kernel(pt_ref, lp_ref, cnt_ref, sl_ref, *refs, npage, tp, hpg, past):
    page_refs = refs[:npage]
    (q_ref, gt_ref, selp_ref, selm_ref, ocmp_ref, kvn_ref, cwin_ref, e_ref, o_ref,
     m_ref, l_ref, acc_ref) = refs[npage:]
    b_id = pl.program_id(0)
    s_id = pl.program_id(1)
    nsteps = pl.num_programs(1)
    tk = npage * PAGE_SIZE
    qpos = past + lax.broadcasted_iota(jnp.int32, (tp, 1), 0)

    @pl.when(s_id == 0)
    def _():
        m_ref[...] = jnp.full(m_ref.shape, NEG, F32)
        l_ref[...] = jnp.zeros(l_ref.shape, F32)
        acc_ref[...] = jnp.zeros(acc_ref.shape, F32)

    @pl.when(s_id * npage < cnt_ref[b_id])
    def _():
        lane = lax.broadcasted_iota(jnp.int32, (1, PAGE_SIZE), 1)
        kpos = jnp.concatenate([lp_ref[b_id, s_id * npage + p] * PAGE_SIZE + lane for p in range(npage)], axis=1)
        d = qpos - kpos
        df = d.astype(F32)
        for g in range(G_B):
            slopes = [sl_ref[g * hpg + r] for r in range(hpg)]
            Q = _stack_heads(q_ref[:, g * hpg * HEAD_DIM:(g + 1) * hpg * HEAD_DIM], hpg)
            slabs = page_refs[0].shape[0] // PAGE_SIZE
            kb = jnp.concatenate([page_refs[p][pl.ds(2 * G_B + g, PAGE_SIZE, stride=slabs), :]
                                  for p in range(npage)], axis=0).astype(BF16)
            vb = jnp.concatenate([page_refs[p][pl.ds(3 * G_B + g, PAGE_SIZE, stride=slabs), :]
                                  for p in range(npage)], axis=0).astype(BF16)
            s = _dot_nt(Q, kb) * SCALE
            mask = (_dot_fewrows(selp_ref[g], e_ref[...]) > 0.5) & (d >= 0)
            ps, alphas = [], []
            for r in range(hpg):
                hr = slice(r * tp, (r + 1) * tp)
                sr = jnp.where(mask, s[hr] - slopes[r] * df, NEG)
                m_old = m_ref[g, hr]
                m_new = jnp.maximum(m_old, sr.max(axis=-1, keepdims=True))
                alpha = jnp.exp(m_old - m_new)
                p = jnp.where(mask, jnp.exp(sr - m_new), 0.0)
                m_ref[g, hr] = m_new
                l_ref[g, hr] = alpha * l_ref[g, hr] + p.sum(axis=-1, keepdims=True)
                ps.append(p)
                alphas.append(jnp.broadcast_to(alpha, (tp, HEAD_DIM)))
            pv = _dot(jnp.concatenate(ps, axis=0).astype(BF16), vb)
            acc_ref[g] = jnp.concatenate(alphas, axis=0) * acc_ref[g] + pv

    @pl.when(s_id == nsteps - 1)
    def _():
        gt = _sigmoid(gt_ref[...])
        npad = HEAD_DIM
        kidx = lax.broadcasted_iota(jnp.int32, (1, npad), 1)
        dn = qpos - (past + kidx)
        new_ok = (kidx < tp) & (dn >= 0)
        zpad = jnp.zeros((npad - tp, HEAD_DIM), F32)
        nblk = past // SEL_BLOCK
        wpast = cwin_ref.shape[0] // (2 * G_B)
        dwp = qpos - (past - wpast + lax.broadcasted_iota(jnp.int32, (1, wpast), 1))
        for g in range(G_B):
            slopes = [sl_ref[g * hpg + r] for r in range(hpg)]
            Q = _stack_heads(q_ref[:, g * hpg * HEAD_DIM:(g + 1) * hpg * HEAD_DIM], hpg)

            def newrows(c):
                blk = kvn_ref[:, (c * G_B + g) * HEAD_DIM:(c * G_B + g + 1) * HEAD_DIM]
                return jnp.concatenate([blk, zpad], axis=0).astype(BF16)

            kn, vn = newrows(2), newrows(3)
            sn = _dot_nt(Q, kn) * SCALE
            mask_n = new_ok & (selm_ref[g][:, nblk:nblk + 1] > 0.5)
            o_sel = []
            for r in range(hpg):
                hr = slice(r * tp, (r + 1) * tp)
                sr = jnp.where(mask_n, sn[hr] - slopes[r] * dn.astype(F32), NEG)
                m_old = m_ref[g, hr]
                m_new = jnp.maximum(m_old, sr.max(axis=-1, keepdims=True))
                alpha = jnp.exp(m_old - m_new)
                p = jnp.where(mask_n, jnp.exp(sr - m_new), 0.0)
                l = alpha * l_ref[g, hr] + p.sum(axis=-1, keepdims=True)
                acc = alpha * acc_ref[g, hr] + _dot_fewrows(p, vn)
                o_sel.append(acc / jnp.maximum(l, 1e-30))
            kwp = cwin_ref[pl.ds(g, wpast, stride=2 * G_B), :].astype(BF16)
            vwp = cwin_ref[pl.ds(G_B + g, wpast, stride=2 * G_B), :].astype(BF16)
            kwn, vwn = newrows(4), newrows(5)
            pieces = [(_dot_nt(Q, kwp) * SCALE, dwp.astype(F32), (dwp >= 0) & (dwp < WINDOW), vwp),
                      (_dot_nt(Q, kwn) * SCALE, dn.astype(F32), new_ok & (dn < WINDOW), vwn)]
            _, o_win = _softmax_pieces(pieces, hpg, tp, slopes)
            o_cmp = ocmp_ref[g]
            for r in range(hpg):
                hr = slice(r * tp, (r + 1) * tp)
                c = 3 * (g * hpg + r)
                col = (g * hpg + r) * HEAD_DIM
                o_ref[:, col:col + HEAD_DIM] = (gt[:, c:c + 1] * o_cmp[hr] + gt[:, c + 1:c + 2] * o_sel[r]
                                                + gt[:, c + 2:c + 3] * o_win[hr])


def _nsa_sample(u, cache, page_table, selm, o_cmp, kvn, cwin, Bs, tp, ts, hpg, past):
    npg = page_table.shape[1]
    npage = SEL_PAGES
    assert npg % npage == 0
    nsteps = npg // npage
    tk = npage * PAGE_SIZE
    nsp = selm.shape[3]
    qw = G_B * hpg * HEAD_DIM
    bpp = PAGE_SIZE // SEL_BLOCK
    e3 = _expand_matrix(nsteps, bpp * npg, tk)
    selm = jnp.where((jnp.arange(tp) < ts)[None, None, :, None], selm, 0.0)
    blocks = selm[..., :bpp * npg].reshape(Bs, G_B, tp, npg, bpp)
    need = jnp.max(blocks, axis=(1, 2, 4)) > 0.5
    order = jnp.argsort(jnp.logical_not(need), axis=1, stable=True).astype(jnp.int32)
    cnt = jnp.sum(need, axis=1).astype(jnp.int32)
    slot = jnp.arange(npg, dtype=jnp.int32)[None]
    last = jnp.take_along_axis(order, jnp.maximum(cnt - 1, 0)[:, None], axis=1)
    lpage = jnp.where(slot < cnt[:, None], order, last)
    ppage = jnp.take_along_axis(page_table, lpage, axis=1)
    selp = jnp.take_along_axis(blocks, lpage[:, None, None, :, None], axis=3)
    selp = jnp.where((slot < cnt[:, None])[:, None, None, :, None], selp, 0.0).reshape(Bs, G_B, tp, bpp * npg)
    gates = jnp.concatenate([u[:, :, qw + MEM_WIDTH + g * HEAD_DIM:qw + MEM_WIDTH + g * HEAD_DIM + 3 * hpg]
                             for g in range(G_B)]
                            + [jnp.zeros((Bs, tp, HEAD_DIM - 3 * hpg * G_B), F32)], axis=2)

    def page_spec(p):
        return pl.BlockSpec((None,) + cache.shape[1:], lambda b, s, pt, lp, cn: (pt[b, s * npage + p], 0, 0))

    def per_batch(shape):
        zeros = (0,) * len(shape)
        return pl.BlockSpec((None,) + shape, lambda b, s, pt, lp, cn: (b,) + zeros)

    grid_spec = pltpu.PrefetchScalarGridSpec(
        num_scalar_prefetch=3,
        grid=(Bs, nsteps),
        in_specs=[pl.BlockSpec(memory_space=pltpu.SMEM)]
        + [page_spec(p) for p in range(npage)]
        + [per_batch((tp, qw)), per_batch((tp, HEAD_DIM)), per_batch((G_B, tp, bpp * npg)),
           per_batch((G_B, tp, nsp)), per_batch((G_B, hpg * tp, HEAD_DIM)), per_batch((tp, kvn.shape[2])),
           per_batch(cwin.shape[1:]),
           pl.BlockSpec((None, bpp * npg, tk), lambda b, s, pt, lp, cn: (s, 0, 0))],
        out_specs=per_batch((tp, qw)),
        scratch_shapes=[pltpu.VMEM((G_B, hpg * tp, 1), F32), pltpu.VMEM((G_B, hpg * tp, 1), F32),
                        pltpu.VMEM((G_B, hpg * tp, HEAD_DIM), F32)],
    )
    return pl.pallas_call(
        functools.partial(_nsa_sample_kernel, npage=npage, tp=tp, hpg=hpg, past=past),
        grid_spec=grid_spec,
        out_shape=jax.ShapeDtypeStruct((Bs, tp, qw), F32),
        compiler_params=_cparams(("parallel", "arbitrary")),
    )(ppage, lpage, cnt, _alibi_slopes(G_B * hpg), *([cache] * npage), u, gates, selp, selm, o_cmp, kvn, cwin, e3)


def _prep_weights(w_in_a, w_in_b, w_o, w_mem_kv, w_kv_b, w_cmp1, w_cmp2, w_ffn_in, w_ffn_out, cmp_pos, hpg):
    tokw = w_in_b.shape[2] - 3 * G_B * hpg - MEM_WIDTH
    D = w_in_b.shape[1]
    wb = w_in_b[0]
    gates = wb[:, tokw:tokw + 3 * G_B * hpg]
    gpad = jnp.zeros((D, HEAD_DIM - 3 * hpg), F32)
    wb2 = jnp.concatenate([wb[:, :tokw], wb[:, tokw + 3 * G_B * hpg:]]
                          + sum([[gates[:, g * 3 * hpg:(g + 1) * 3 * hpg], gpad] for g in range(G_B)], []),
                          axis=1)
    r = CMP_BLOCK // CMP_STRIDE
    flat = CMP_STRIDE * HEAD_DIM
    wc = w_cmp1.reshape(2, r, flat, HEAD_DIM).transpose(0, 2, 1, 3).reshape(2, flat, r * HEAD_DIM)
    pe2 = jnp.concatenate([cmp_pos.reshape(2, r, flat), jnp.zeros((2, 16 - r, flat), F32)], axis=1)
    return dict(w_in_a=w_in_a.astype(BF16), w_in_b=wb2.astype(BF16), w_o=w_o.astype(BF16),
                w_mem_kv=w_mem_kv.astype(BF16), w_kv_b=w_kv_b.astype(BF16), wc=wc.astype(BF16),
                w2=w_cmp2.astype(BF16), w_ffn_in=w_ffn_in.astype(BF16), w_ffn_out=w_ffn_out.astype(BF16),
                pe2=pe2)


def _layer_tail(h, mix, mo, l, prm, conv_buf, B, T):
    g = prm['norm_gains'][l]
    h, xn = _mmres(mix, mo, (prm['w_o'], l), h, g[1], g_next=g[2])
    y, new_buf = _ffn_in(xn, (prm['w_ffn_in'], l), prm['w_ffn_conv'][l], prm['b_ffn_conv'][l],
                         None if conv_buf is None else conv_buf[l], B, T)
    h = _mmres(y, None, (prm['w_ffn_out'], l), h, g[3])
    return h, new_buf


def _trunk(x, mem_kv, hgrn_s0, conv_buf, prm, B, T, t_valid, nsa_fn):
    D = x.shape[-1]
    H = prm['hgrn_norm'].shape[1]
    h = x.reshape(B * T, D)
    u = _mm(h, prm['norm_gains'][0, 0], (prm['w_in_a'], 0)).reshape(B, T, -1)
    mix, s_fin = _hgrn(u, prm['lb_logits'], prm['hgrn_norm'][0], None if hgrn_s0 is None else hgrn_s0[0],
                       B, T, H, t_valid)
    mo = _memattn(u, 4 * H * HEAD_DIM // MEM_WIDTH, mem_kv, 0, B, T)
    h, buf0 = _layer_tail(h, mix.reshape(B * T, -1), mo.reshape(B * T, -1), 0, prm, conv_buf, B, T)
    n_slab = prm['w_kv_b'].shape[1] // HEAD_DIM
    kv, kv_rows, kv_win = _mm(h, prm['kv_norm'], prm['w_kv_b'], slabs=(n_slab - 2 * G_B, 2 * G_B))
    u = _mm(h, prm['norm_gains'][1, 0], prm['w_in_b'])
    mix = nsa_fn(u, kv)
    mo = _memattn(u.reshape(B, T, -1), H * HEAD_DIM // MEM_WIDTH, mem_kv, 1, B, T)
    h, buf1 = _layer_tail(h, mix.reshape(B * T, -1), mo.reshape(B * T, -1), 1, prm, conv_buf, B, T)
    return h.reshape(B, T, D), s_fin, (buf0, buf1), kv_rows, kv_win


def kernel(x_prompt, x_sample, mem_prompt, state_hgrn, cache_conv, cache_mem, cache_kv, cache_win,
           page_table, norm_gains, w_in_a, lb_logits, hgrn_norm, w_in_b, w_o, w_mem_kv, kv_norm,
           w_kv_b, cmp_pos, w_cmp1, w_cmp2, w_ffn_in, w_ffn_conv, b_ffn_conv, w_ffn_out):
    Bp, Tp, D = x_prompt.shape
    Bs, Ts, _ = x_sample.shape
    depth = norm_gains.shape[0]
    assert depth == 2 and w_in_a.shape[0] == 1 and w_in_b.shape[0] == 1
    H = hgrn_norm.shape[1]
    hpg = H // G_B
    ml = mem_prompt.shape[1]
    n_rows = cache_kv.shape[2]

    prm = _prep_weights(w_in_a, w_in_b, w_o, w_mem_kv, w_kv_b, w_cmp1, w_cmp2, w_ffn_in, w_ffn_out,
                        cmp_pos, hpg)
    prm.update(norm_gains=norm_gains, lb_logits=lb_logits, hgrn_norm=hgrn_norm, kv_norm=kv_norm,
               w_ffn_conv=w_ffn_conv, b_ffn_conv=b_ffn_conv)
    cb = _cmp_bias(prm['pe2'], prm['wc'])

    memx = mem_prompt.reshape(Bp * ml, D)
    mem_kv_p = [_mm(memx, None, (prm['w_mem_kv'], l), norm=False).reshape(Bp, ml, 2 * MEM_WIDTH)
                for l in range(depth)]

    def nsa_p(u, kv):
        kcv = _cmp_prompt(kv, prm['wc'], prm['w2'], cb, Bp, Tp)
        return _nsa_prompt(u, kv, kcv, Bp, Tp, hpg)

    y_p, hgrn_p, conv_p, kvr_p, kvw_p = _trunk(x_prompt, mem_kv_p, None, None, prm, Bp, Tp, Tp, nsa_p)

    tp = -(-Ts // SUBLANES) * SUBLANES
    past = page_table.shape[1] * PAGE_SIZE
    assert Ts < CMP_STRIDE and tp <= SEL_BLOCK and past >= WINDOW
    xs = jnp.concatenate([x_sample, jnp.zeros((Bs, tp - Ts, D), F32)], axis=1)

    assert n_rows == 4
    cache5 = cache_kv.reshape(cache_kv.shape[0], PAGE_SIZE * n_rows * G_B, HEAD_DIM)

    def nsa_s(u, kv):
        u3 = u.reshape(Bs, tp, -1)
        part = _cmp_pages(cache5, page_table, prm['wc'])
        o_cmp, selm = _nsa_sample_cmp(u3, part, cb, prm['w2'], Bs, tp, hpg, past)
        return _nsa_sample(u3, cache5, page_table, selm, o_cmp, kv.reshape(Bs, tp, -1),
                           cache_win.reshape(Bs, -1, HEAD_DIM), Bs, tp, Ts, hpg, past)

    y_s, hgrn_s, conv_s, kvr_s, kvw_s = _trunk(xs, cache_mem.reshape(depth, Bs, -1, HEAD_DIM), state_hgrn, cache_conv,
                                       prm, Bs, tp, Ts, nsa_s)

    wl = min(WINDOW, Tp)
    return (
        y_p,
        y_s[:, :Ts],
        hgrn_p[None],
        hgrn_s[None],
        jnp.stack(conv_p),
        jnp.stack([c[:, Ts - (CONV_W - 1):Ts] for c in conv_s]),
        jnp.stack(mem_kv_p).reshape(depth, Bp, ml, 2, MEM_HEADS, HEAD_DIM),
        kvr_p.reshape(Bp, Tp // PAGE_SIZE, PAGE_SIZE, n_rows, G_B, HEAD_DIM),
        kvr_s.reshape(Bs, tp, n_rows, G_B, HEAD_DIM)[:, :Ts],
        kvw_p.reshape(Bp, Tp, 2, G_B, HEAD_DIM)[:, Tp - wl:],
        kvw_s.reshape(Bs, tp, 2, G_B, HEAD_DIM)[:, :Ts],
    )
```
